```python
import numpy as np
import jax
import jax.numpy as jnp
from jax import lax

D_MODEL = 1024
BATCH = 16
SEQ = 2048
DEPTH = 2

MLSTM_HEADS = 4
MLSTM_QK_DIM = 64
MLSTM_V_DIM = 128
MLSTM_CHUNK = 64
MLSTM_WIDTH = MLSTM_HEADS * MLSTM_V_DIM
SCONV_WIDTH = D_MODEL - MLSTM_WIDTH
SCONV_TAPS = 3
CONF_WIDTH = D_MODEL
CONF_TAPS = 31
N_EXPERTS = 32
TOP_K = 4
D_EXPERT = D_MODEL
SWIGLU_LIMIT = 7.0
SWIGLU_ALPHA = 1.702
EXPERT_ROWS = 128
LN_EPS = 1e-5
HEAD_NORM_EPS = 1e-6
DEEPNORM_ALPHA = (2 * DEPTH) ** 0.25
DEEPNORM_BETA = (8 * DEPTH) ** -0.25
N_EVEN = (DEPTH + 1) // 2
N_ODD = DEPTH // 2
AB_SPLIT_SIZES = (MLSTM_HEADS * MLSTM_QK_DIM, MLSTM_HEADS * MLSTM_QK_DIM, MLSTM_WIDTH, MLSTM_WIDTH,
                  MLSTM_HEADS, MLSTM_HEADS, SCONV_WIDTH, SCONV_WIDTH, SCONV_WIDTH)
AB_IN_WIDTH = sum(AB_SPLIT_SIZES)

kernel_name = 'hybrid_mlstm_shortconv_conformer_moe_deepnorm'


def layer_norm(x, g, b, eps=LN_EPS):
    xf = x.astype(jnp.float32)
    mu = jnp.mean(xf, axis=-1, keepdims=True)
    var = jnp.mean(jnp.square(xf - mu), axis=-1, keepdims=True)
    y = (xf - mu) * lax.rsqrt(var + eps) * g.astype(jnp.float32) + b.astype(jnp.float32)
    return y.astype(x.dtype)


def causal_depthwise_conv(u, w):
    taps, ch = w.shape
    return lax.conv_general_dilated(u, w[:, None, :].astype(u.dtype), window_strides=(1,),
                                    padding=[(taps - 1, 0)],
                                    dimension_numbers=('NWC', 'WIO', 'NWC'),
                                    feature_group_count=ch)


def mlstm_chunkwise(q, k, v, i_pre, f_pre):
    Bn, S, H, dk = q.shape
    dv = v.shape[-1]
    L = MLSTM_CHUNK
    NC = S // L
    q = q.reshape(Bn, NC, L, H, dk).transpose(0, 3, 1, 2, 4)
    k = k.reshape(Bn, NC, L, H, dk).transpose(0, 3, 1, 2, 4) * (dk ** -0.5)
    v = v.reshape(Bn, NC, L, H, dv).transpose(0, 3, 1, 2, 4)
    ig = i_pre.reshape(Bn, NC, L, H).transpose(0, 3, 1, 2)
    logf = jax.nn.log_sigmoid(f_pre).reshape(Bn, NC, L, H).transpose(0, 3, 1, 2)
    b = jnp.cumsum(logf, axis=-1)
    g = b[..., -1]
    a = g[..., None] - b + ig

    def step(carry, inp):
        C, n, m = carry
        k_c, v_c, a_c, g_c = inp
        m_new = jnp.maximum(g_c + m, jnp.max(a_c, axis=-1))
        decay = jnp.exp(g_c + m - m_new)
        w = jnp.exp(a_c - m_new[..., None])
        C_new = decay[..., None, None] * C + jnp.einsum('bhl,bhlk,bhlv->bhkv', w, k_c, v_c)
        n_new = decay[..., None] * n + jnp.einsum('bhl,bhlk->bhk', w, k_c)
        return (C_new, n_new, m_new), (C, n, m)

    init = (jnp.zeros((Bn, H, dk, dv), jnp.float32), jnp.zeros((Bn, H, dk), jnp.float32),
            jnp.zeros((Bn, H), jnp.float32))
    xs = (k.transpose(2, 0, 1, 3, 4), v.transpose(2, 0, 1, 3, 4), a.transpose(2, 0, 1, 3), g.transpose(2, 0, 1))
    _, (C_prev, n_prev, m_prev) = lax.scan(step, init, xs)
    C_prev = C_prev.transpose(1, 2, 0, 3, 4)
    n_prev = n_prev.transpose(1, 2, 0, 3)
    m_prev = m_prev.transpose(1, 2, 0)

    causal = jnp.tril(jnp.ones((L, L), dtype=bool))
    dmat = b[..., :, None] - b[..., None, :] + ig[..., None, :]
    dmat = jnp.where(causal, dmat, -jnp.inf)
    e_inter = b + m_prev[..., None]
    m_t = jnp.maximum(e_inter, jnp.max(dmat, axis=-1))
    w_intra = jnp.exp(dmat - m_t[..., None])
    w_inter = jnp.exp(e_inter - m_t)
    s = jnp.einsum('bhctk,bhcsk->bhcts', q, k) * w_intra
    num = w_inter[..., None] * jnp.einsum('bhctk,bhckv->bhctv', q, C_prev) + jnp.einsum('bhcts,bhcsv->bhctv', s, v)
    den = w_inter * jnp.einsum('bhctk,bhck->bhct', q, n_prev) + jnp.sum(s, axis=-1)
    h = num / jnp.maximum(jnp.abs(den), jnp.exp(-m_t))[..., None]
    return h.transpose(0, 2, 3, 1, 4).reshape(Bn, S, H, dv)


def mixer_ab(x, w_in, b_igate, b_fgate, conv_w, head_gain, w_out):
    Bn, S, _ = x.shape
    H, dk, dv = MLSTM_HEADS, MLSTM_QK_DIM, MLSTM_V_DIM
    f32 = jnp.float32
    z = x @ w_in
    cuts = np.cumsum(AB_SPLIT_SIZES)[:-1].tolist()
    q, k, v, o_pre, i_pre, f_pre, sb, sc, sx = jnp.split(z, cuts, axis=-1)
    h = mlstm_chunkwise(q.reshape(Bn, S, H, dk).astype(f32), k.reshape(Bn, S, H, dk).astype(f32),
                        v.reshape(Bn, S, H, dv).astype(f32), (i_pre + b_igate).astype(f32),
                        (f_pre + b_fgate).astype(f32))
    mu = jnp.mean(h, axis=-1, keepdims=True)
    var = jnp.mean(jnp.square(h - mu), axis=-1, keepdims=True)
    h = (h - mu) * lax.rsqrt(var + HEAD_NORM_EPS) * head_gain.reshape(H, dv).astype(f32)
    h = (jax.nn.sigmoid(o_pre.astype(f32)) * h.reshape(Bn, S, MLSTM_WIDTH)).astype(x.dtype)
    y = sb * causal_depthwise_conv(sc * sx, conv_w)
    return jnp.concatenate([h, y], axis=-1) @ w_out


def conformer_conv(x, w_pw1, w_dw, b_dw, ln_g, ln_b, w_pw2):
    a, gate = jnp.split(x @ w_pw1, 2, axis=-1)
    u = a * jax.nn.sigmoid(gate)
    u = causal_depthwise_conv(u, w_dw) + b_dw
    u = jax.nn.silu(layer_norm(u, ln_g, ln_b))
    return u @ w_pw2


def moe(x, router_w, router_b, w_up, b_up, w_down, b_down):
    Bn, S, D = x.shape
    T = Bn * S
    E, K, R = N_EXPERTS, TOP_K, EXPERT_ROWS
    xt = x.reshape(T, D)
    logits = (xt @ router_w + router_b).astype(jnp.float32)
    top_vals, top_idx = lax.top_k(logits, K)
    gates = jax.nn.softmax(top_vals, axis=-1)
    A = T * K
    e_flat = top_idx.reshape(A).astype(jnp.int32)
    tok_flat = jnp.repeat(jnp.arange(T, dtype=jnp.int32), K)
    g_flat = gates.reshape(A)
    order = jnp.argsort(e_flat)
    e_sorted = e_flat[order]
    counts = jnp.bincount(e_flat, length=E).astype(jnp.int32)
    starts = jnp.cumsum(counts) - counts
    padded = ((counts + R - 1) // R) * R
    pends = jnp.cumsum(padded)
    pstarts = pends - padded
    dest = pstarts[e_sorted] + (jnp.arange(A, dtype=jnp.int32) - starts[e_sorted])
    P = -(-(A + E * (R - 1)) // R) * R
    nb = P // R
    row_tok = jnp.full((P,), T, jnp.int32).at[dest].set(tok_flat[order])
    row_gate = jnp.zeros((P,), x.dtype).at[dest].set(g_flat[order].astype(x.dtype))
    group_expert = jnp.minimum(jnp.searchsorted(pends, jnp.arange(nb, dtype=jnp.int32) * R, side='right'),
                               E - 1).astype(jnp.int32)
    x_pad = jnp.concatenate([xt, jnp.zeros((1, D), xt.dtype)], axis=0)
    xs = x_pad[row_tok].reshape(nb, R, D)

    def expert_rows(args):
        xb, e = args
        hb = xb @ w_up[e] + b_up[e]
        glu, lin = jnp.split(hb, 2, axis=-1)
        glu = jnp.minimum(glu, SWIGLU_LIMIT)
        lin = jnp.clip(lin, -SWIGLU_LIMIT, SWIGLU_LIMIT)
        act = glu * jax.nn.sigmoid(SWIGLU_ALPHA * glu) * (lin + 1.0)
        return act @ w_down[e] + b_down[e]

    ys = lax.map(expert_rows, (xs, group_expert)).reshape(P, D)
    out = jnp.zeros((T + 1, D), x.dtype).at[row_tok].add(ys * row_gate[:, None])[:T]
    return out.reshape(Bn, S, D)


def setup_inputs(seed: int = 0) -> dict:
    key = jax.random.key(seed)
    ks = jax.random.split(key, 24)
    f32 = jnp.float32

    def nrm(k, shape, scale):
        return jax.random.normal(k, shape, f32) * scale

    D, H, E, F = D_MODEL, MLSTM_HEADS, N_EXPERTS, D_EXPERT
    beta = DEEPNORM_BETA
    x = nrm(ks[0], (BATCH, SEQ, D), 1.0)
    col_scale = jnp.concatenate([
        jnp.ones((2 * H * MLSTM_QK_DIM,), f32), jnp.full((MLSTM_WIDTH,), beta, f32),
        jnp.ones((MLSTM_WIDTH + 2 * H + 2 * SCONV_WIDTH,), f32), jnp.full((SCONV_WIDTH,), beta, f32)])
    ab_w_in = nrm(ks[1], (N_EVEN, D, AB_IN_WIDTH), D ** -0.5) * col_scale
    ab_b_igate = nrm(ks[2], (N_EVEN, H), 0.1)
    ab_b_fgate = jnp.linspace(3.0, 6.0, H, dtype=f32) + nrm(ks[3], (N_EVEN, H), 0.1)
    ab_conv_w = nrm(ks[4], (N_EVEN, SCONV_TAPS, SCONV_WIDTH), SCONV_TAPS ** -0.5)
    ab_head_gain = 1.0 + nrm(ks[5], (N_EVEN, MLSTM_WIDTH), 0.02)
    ab_w_out = nrm(ks[6], (N_EVEN, MLSTM_WIDTH + SCONV_WIDTH, D), (MLSTM_WIDTH + SCONV_WIDTH) ** -0.5 * beta)
    glu_scale = jnp.concatenate([jnp.full((CONF_WIDTH,), beta, f32), jnp.ones((CONF_WIDTH,), f32)])
    cf_w_pw1 = nrm(ks[7], (N_ODD, D, 2 * CONF_WIDTH), D ** -0.5) * glu_scale
    cf_w_dw = nrm(ks[8], (N_ODD, CONF_TAPS, CONF_WIDTH), CONF_TAPS ** -0.5)
    cf_b_dw = nrm(ks[9], (N_ODD, CONF_WIDTH), 0.02)
    cf_ln_g = 1.0 + nrm(ks[10], (N_ODD, CONF_WIDTH), 0.02)
    cf_ln_b = nrm(ks[11], (N_ODD, CONF_WIDTH), 0.02)
    cf_w_pw2 = nrm(ks[12], (N_ODD, CONF_WIDTH, D), CONF_WIDTH ** -0.5 * beta)
    router_w = nrm(ks[13], (DEPTH, D, E), D ** -0.5)
    router_b = nrm(ks[14], (DEPTH, E), 0.01)
    exp_w_up = nrm(ks[15], (DEPTH, E, D, 2 * F), D ** -0.5 * beta)
    exp_b_up = nrm(ks[16], (DEPTH, E, 2 * F), 0.02)
    exp_w_down = nrm(ks[17], (DEPTH, E, F, D), F ** -0.5 * beta)
    exp_b_down = nrm(ks[18], (DEPTH, E, D), 0.02)
    post_ln_g = 1.0 + nrm(ks[19], (DEPTH, 2, D), 0.02)
    post_ln_b = nrm(ks[20], (DEPTH, 2, D), 0.02)
    return {'x': x, 'ab_w_in': ab_w_in, 'ab_b_igate': ab_b_igate, 'ab_b_fgate': ab_b_fgate,
            'ab_conv_w': ab_conv_w, 'ab_head_gain': ab_head_gain, 'ab_w_out': ab_w_out,
            'cf_w_pw1': cf_w_pw1, 'cf_w_dw': cf_w_dw, 'cf_b_dw': cf_b_dw, 'cf_ln_g': cf_ln_g,
            'cf_ln_b': cf_ln_b, 'cf_w_pw2': cf_w_pw2, 'router_w': router_w, 'router_b': router_b,
            'exp_w_up': exp_w_up, 'exp_b_up': exp_b_up, 'exp_w_down': exp_w_down, 'exp_b_down': exp_b_down,
            'post_ln_g': post_ln_g, 'post_ln_b': post_ln_b}


def reference(x, ab_w_in, ab_b_igate, ab_b_fgate, ab_conv_w, ab_head_gain, ab_w_out,
              cf_w_pw1, cf_w_dw, cf_b_dw, cf_ln_g, cf_ln_b, cf_w_pw2,
              router_w, router_b, exp_w_up, exp_b_up, exp_w_down, exp_b_down,
              post_ln_g, post_ln_b):
    for layer in range(DEPTH):
        j = layer // 2
        if layer % 2 == 0:
            h = mixer_ab(x, ab_w_in[j], ab_b_igate[j], ab_b_fgate[j], ab_conv_w[j], ab_head_gain[j], ab_w_out[j])
        else:
            h = conformer_conv(x, cf_w_pw1[j], cf_w_dw[j], cf_b_dw[j], cf_ln_g[j], cf_ln_b[j], cf_w_pw2[j])
        x = layer_norm(DEEPNORM_ALPHA * x + h, post_ln_g[layer, 0], post_ln_b[layer, 0])
        f = moe(x, router_w[layer], router_b[layer], exp_w_up[layer], exp_b_up[layer],
                exp_w_down[layer], exp_b_down[layer])
        x = layer_norm(DEEPNORM_ALPHA * x + f, post_ln_g[layer, 1], post_ln_b[layer, 1])
    return x
```

```python
import functools

import jax
import jax.numpy as jnp
from jax import lax
from jax.experimental import pallas as pl
from jax.experimental.pallas import tpu as pltpu

F32 = jnp.float32
BF16 = jnp.bfloat16

LANES = 128
SUBLANES = 8

MLSTM_HEADS = 4
MLSTM_QK_DIM = 64
MLSTM_V_DIM = 128
TOP_K = 4
SWIGLU_LIMIT = 7.0
SWIGLU_ALPHA = 1.702
LN_EPS = 1e-5
HEAD_NORM_EPS = 1e-6

SEQ_TILE = 512
MLSTM_CHUNK = 128
ROUTER_TILE = 512
DISPATCH_TILE = 256
COMBINE_TILE = 128
EXPERT_ROWS = 256
CONV_ROWS = 64
VMEM_LIMIT = 56 * 1024 * 1024


def _layer_norm(y, g, b):
    mu = jnp.mean(y, axis=-1, keepdims=True)
    yc = y - mu
    var = jnp.mean(yc * yc, axis=-1, keepdims=True)
    return yc * lax.rsqrt(var + LN_EPS) * g + b


def _log_sigmoid(x):
    return jnp.minimum(x, 0.0) - jnp.log1p(jnp.exp(-jnp.abs(x)))


def _dot(a, b):
    return jnp.dot(a, b, preferred_element_type=F32)


def _dot_nt(a, b):
    return lax.dot_general(a, b, (((1,), (1,)), ((), ())), preferred_element_type=F32)


def _dot_tn(a, b):
    return lax.dot_general(a, b, (((0,), (0,)), ((), ())), preferred_element_type=F32)


def _dot_f32(a, b):
    return jnp.dot(a, b, preferred_element_type=F32, precision=lax.Precision.HIGHEST)


def _mixer_ab_kernel(x_ref, wall_ref, wift_ref, bcol_ref, brow_ref, convw_ref, gain_ref, wout_ref,
                     lng_ref, lnb_ref, out_ref,
                     z_ref, hy_ref, ubuf_ref, c_ref, n_ref, m_ref, *, alpha, sconv_w):
    H, L, TS = MLSTM_HEADS, MLSTM_CHUNK, SEQ_TILE
    HW = H * LANES
    q0, k0, v0, o0 = 0, HW, 2 * HW, 3 * HW
    sb0 = 4 * HW
    sc0 = sb0 + sconv_w
    sx0 = sc0 + sconv_w
    if0 = sx0 + sconv_w

    @pl.when(pl.program_id(1) == 0)
    def _():
        c_ref[...] = jnp.zeros_like(c_ref)
        n_ref[...] = jnp.zeros_like(n_ref)
        m_ref[...] = jnp.zeros_like(m_ref)
        ubuf_ref[0:SUBLANES, :] = jnp.zeros((SUBLANES, sconv_w), F32)

    x = x_ref[...]
    xb = x.astype(BF16)
    z_ref[...] = _dot(xb, wall_ref[...])
    pre_row = _dot_nt(wift_ref[...], xb) + brow_ref[...]
    logf_row = _log_sigmoid(pre_row)
    pre_col = z_ref[:, if0:if0 + LANES] + bcol_ref[...]
    logf_col = _log_sigmoid(pre_col)

    r_i = lax.broadcasted_iota(jnp.int32, (L, L), 0)
    c_i = lax.broadcasted_iota(jnp.int32, (L, L), 1)
    causal = r_i >= c_i
    tril = causal.astype(F32)
    triu = (r_i <= c_i).astype(F32)

    for c in range(TS // L):
        r0 = c * L
        b_col = _dot_f32(tril, logf_col[r0:r0 + L, :])
        b_row = _dot_f32(logf_row[:, r0:r0 + L], triu)
        for h in range(H):
            q = z_ref[r0:r0 + L, q0 + h * LANES:q0 + (h + 1) * LANES]
            k = z_ref[r0:r0 + L, k0 + h * LANES:k0 + (h + 1) * LANES]
            v = z_ref[r0:r0 + L, v0 + h * LANES:v0 + (h + 1) * LANES]
            o = z_ref[r0:r0 + L, o0 + h * LANES:o0 + (h + 1) * LANES]
            qb, kb, vb = q.astype(BF16), k.astype(BF16), v.astype(BF16)
            bc = b_col[:, H + h:H + h + 1]
            ic = pre_col[r0:r0 + L, h:h + 1]
            br = b_row[H + h:H + h + 1, :]
            ir = pre_row[h:h + 1, r0:r0 + L]
            c_prev = c_ref[h]
            n_prev = n_ref[h]
            m_prev = m_ref[h][:, 0:1]

            dmat = jnp.where(causal, bc - br + ir, -jnp.inf)
            e_inter = bc + m_prev
            m_t = jnp.maximum(e_inter, jnp.max(dmat, axis=-1, keepdims=True))
            w_intra = jnp.exp(dmat - m_t)
            w_inter = jnp.exp(e_inter - m_t)
            s = _dot_nt(qb, kb) * w_intra
            num = w_inter * _dot(qb, c_prev.astype(BF16)) + _dot(s.astype(BF16), vb)
            den = (w_inter * jnp.sum(q * n_prev, axis=-1, keepdims=True)
                   + jnp.sum(s, axis=-1, keepdims=True))
            hh = num / jnp.maximum(jnp.abs(den), jnp.exp(-m_t))

            g = bc[L - 1:L, :]
            a_col = g - bc + ic
            m_new = jnp.maximum(g + m_prev, jnp.max(a_col, axis=0, keepdims=True))
            decay = jnp.exp(g + m_prev - m_new)
            kw = k * jnp.exp(a_col - m_new)
            c_ref[h] = decay * c_prev + _dot_tn(kw.astype(BF16), vb)
            n_ref[h] = decay * n_prev + jnp.sum(kw, axis=0, keepdims=True)
            m_ref[h] = jnp.broadcast_to(m_new, (1, LANES))

            mu = jnp.mean(hh, axis=-1, keepdims=True)
            hc = hh - mu
            var = jnp.mean(hc * hc, axis=-1, keepdims=True)
            hn = hc * lax.rsqrt(var + HEAD_NORM_EPS) * gain_ref[:, h * LANES:(h + 1) * LANES]
            hy_ref[r0:r0 + L, h * LANES:(h + 1) * LANES] = (jax.nn.sigmoid(o) * hn).astype(BF16)

    u = z_ref[:, sc0:sc0 + sconv_w] * z_ref[:, sx0:sx0 + sconv_w]
    ubuf_ref[SUBLANES:SUBLANES + TS, :] = u
    conv = (convw_ref[2:3, :] * u
            + convw_ref[1:2, :] * ubuf_ref[SUBLANES - 1:SUBLANES - 1 + TS, :]
            + convw_ref[0:1, :] * ubuf_ref[SUBLANES - 2:SUBLANES - 2 + TS, :])
    hy_ref[:, HW:HW + sconv_w] = (z_ref[:, sb0:sb0 + sconv_w] * conv).astype(BF16)
    ubuf_ref[0:SUBLANES, :] = ubuf_ref[TS:TS + SUBLANES, :]

    mix = _dot(hy_ref[...], wout_ref[...])
    out_ref[...] = _layer_norm(alpha * x + mix, lng_ref[...], lnb_ref[...])


def _mixer_ab(x, w_in, b_igate, b_fgate, conv_w, head_gain, w_out, ln_g, ln_b, alpha):
    Bn, S, D = x.shape
    H, dk, dv = MLSTM_HEADS, MLSTM_QK_DIM, MLSTM_V_DIM
    assert dv == LANES and dk <= LANES and S % SEQ_TILE == 0 and SEQ_TILE % MLSTM_CHUNK == 0
    hw = H * dv
    sw = D - hw
    cuts = [H * dk, 2 * H * dk, 2 * H * dk + hw, 2 * H * dk + 2 * hw]
    wq, wk, wv, wo = (w_in[:, a:b] for a, b in zip([0] + cuts[:-1], cuts))
    wi = w_in[:, cuts[-1]:cuts[-1] + H]
    wf = w_in[:, cuts[-1] + H:cuts[-1] + 2 * H]
    ws = w_in[:, cuts[-1] + 2 * H:]

    def pad_heads(w, scale):
        w = (w * scale).reshape(D, H, dk)
        return jnp.pad(w, ((0, 0), (0, 0), (0, LANES - dk))).reshape(D, H * LANES)

    w_if = jnp.concatenate([wi, wf], axis=1)
    w_all = jnp.concatenate(
        [pad_heads(wq, 1.0), pad_heads(wk, dk ** -0.5), wv, wo, ws,
         jnp.pad(w_if, ((0, 0), (0, LANES - 2 * H)))], axis=1).astype(BF16)
    nz = w_all.shape[1]
    w_if_t = w_if.T.astype(BF16)
    b_if = jnp.concatenate([b_igate, b_fgate]).astype(F32)
    b_col = jnp.pad(b_if, (0, LANES - 2 * H)).reshape(1, LANES)
    b_row = b_if.reshape(2 * H, 1)

    const = lambda b, j: (0, 0)
    kern = functools.partial(_mixer_ab_kernel, alpha=alpha, sconv_w=sw)
    return pl.pallas_call(
        kern,
        grid=(Bn, S // SEQ_TILE),
        in_specs=[
            pl.BlockSpec((None, SEQ_TILE, D), lambda b, j: (b, j, 0)),
            pl.BlockSpec((D, nz), const),
            pl.BlockSpec((2 * H, D), const),
            pl.BlockSpec((1, LANES), const),
            pl.BlockSpec((2 * H, 1), const),
            pl.BlockSpec((conv_w.shape[0], sw), const),
            pl.BlockSpec((1, hw), const),
            pl.BlockSpec((D, D), const),
            pl.BlockSpec((1, D), const),
            pl.BlockSpec((1, D), const),
        ],
        out_specs=pl.BlockSpec((None, SEQ_TILE, D), lambda b, j: (b, j, 0)),
        out_shape=jax.ShapeDtypeStruct((Bn, S, D), F32),
        scratch_shapes=[
            pltpu.VMEM((SEQ_TILE, nz), F32),
            pltpu.VMEM((SEQ_TILE, D), BF16),
            pltpu.VMEM((SEQ_TILE + 2 * SUBLANES, sw), F32),
            pltpu.VMEM((H, LANES, dv), F32),
            pltpu.VMEM((H, 1, LANES), F32),
            pltpu.VMEM((H, 1, LANES), F32),
        ],
        compiler_params=pltpu.CompilerParams(
            dimension_semantics=("parallel", "arbitrary"), vmem_limit_bytes=VMEM_LIMIT),
        name="mixer_ab",
    )(x, w_all, w_if_t, b_col, b_row, conv_w.astype(F32), head_gain.reshape(1, hw).astype(F32),
      w_out.astype(BF16), ln_g.reshape(1, D), ln_b.reshape(1, D))


def _conformer_kernel(x_ref, w1_ref, wdw_ref, bdw_ref, cg_ref, cb_ref, w2_ref, lng_ref, lnb_ref,
                      out_ref, ubuf_ref, uc_ref, *, alpha, taps, hist):
    TS = SEQ_TILE
    C = uc_ref.shape[1]

    @pl.when(pl.program_id(1) == 0)
    def _():
        ubuf_ref[0:hist, :] = jnp.zeros((hist, C), F32)

    x = x_ref[...]
    z = _dot(x.astype(BF16), w1_ref[...])
    ubuf_ref[hist:hist + TS, :] = z[:, :C] * jax.nn.sigmoid(z[:, C:])

    base = hist - (taps - 1)
    for rb in range(TS // CONV_ROWS):
        for cb in range(C // LANES):
            cs = slice(cb * LANES, (cb + 1) * LANES)
            acc = jnp.zeros((CONV_ROWS, LANES), F32) + bdw_ref[:, cs]
            for j in range(taps):
                start = rb * CONV_ROWS + base + j
                acc = acc + wdw_ref[j:j + 1, cs] * ubuf_ref[start:start + CONV_ROWS, cs]
            uc_ref[rb * CONV_ROWS:(rb + 1) * CONV_ROWS, cs] = acc
    ubuf_ref[0:hist, :] = ubuf_ref[TS:TS + hist, :]

    un = _layer_norm(uc_ref[...], cg_ref[...], cb_ref[...])
    act = un * jax.nn.sigmoid(un)
    mix = _dot(act.astype(BF16), w2_ref[...])
    out_ref[...] = _layer_norm(alpha * x + mix, lng_ref[...], lnb_ref[...])


def _conformer(x, w_pw1, w_dw, b_dw, cg, cb, w_pw2, ln_g, ln_b, alpha):
    Bn, S, D = x.shape
    taps, C = w_dw.shape
    hist = -(-(taps - 1) // SUBLANES) * SUBLANES
    assert S % SEQ_TILE == 0 and C % LANES == 0 and hist <= SEQ_TILE
    const = lambda b, j: (0, 0)
    kern = functools.partial(_conformer_kernel, alpha=alpha, taps=taps, hist=hist)
    return pl.pallas_call(
        kern,
        grid=(Bn, S // SEQ_TILE),
        in_specs=[
            pl.BlockSpec((None, SEQ_TILE, D), lambda b, j: (b, j, 0)),
            pl.BlockSpec((D, 2 * C), const),
            pl.BlockSpec((taps, C), const),
            pl.BlockSpec((1, C), const),
            pl.BlockSpec((1, C), const),
            pl.BlockSpec((1, C), const),
            pl.BlockSpec((C, D), const),
            pl.BlockSpec((1, D), const),
            pl.BlockSpec((1, D), const),
        ],
        out_specs=pl.BlockSpec((None, SEQ_TILE, D), lambda b, j: (b, j, 0)),
        out_shape=jax.ShapeDtypeStruct((Bn, S, D), F32),
        scratch_shapes=[
            pltpu.VMEM((SEQ_TILE + hist, C), F32),
            pltpu.VMEM((SEQ_TILE, C), F32),
        ],
        compiler_params=pltpu.CompilerParams(
            dimension_semantics=("parallel", "arbitrary"), vmem_limit_bytes=VMEM_LIMIT),
        name="conformer",
    )(x, w_pw1.astype(BF16), w_dw.astype(F32), b_dw.reshape(1, C), cg.reshape(1, C),
      cb.reshape(1, C), w_pw2.astype(BF16), ln_g.reshape(1, D), ln_b.reshape(1, D))


def _router_kernel(x_ref, rwt_ref, rb_ref, idx_ref, gate_ref, rank_ref, cnt_ref, base_ref):
    E = rwt_ref.shape[0]
    TT = x_ref.shape[0]

    @pl.when(pl.program_id(0) == 0)
    def _():
        base_ref[...] = jnp.zeros_like(base_ref)

    logits = lax.dot_general(rwt_ref[...], x_ref[...], (((1,), (1,)), ((), ())),
                             preferred_element_type=F32,
                             precision=lax.Precision.HIGHEST) + rb_ref[...]
    e_iota = lax.broadcasted_iota(jnp.int32, (E, TT), 0)
    work = logits
    vals, idxs, hots = [], [], []
    for _ in range(TOP_K):
        mx = jnp.max(work, axis=0, keepdims=True)
        ix = jnp.min(jnp.where(work == mx, e_iota, E), axis=0, keepdims=True)
        hot = e_iota == ix
        work = jnp.where(hot, -jnp.inf, work)
        vals.append(mx)
        idxs.append(ix)
        hots.append(hot)
    ex = [jnp.exp(v - vals[0]) for v in vals]
    tot = functools.reduce(lambda a, b: a + b, ex)

    member = functools.reduce(lambda a, b: a | b, hots).astype(F32)
    s_i = lax.broadcasted_iota(jnp.int32, (TT, TT), 0)
    t_i = lax.broadcasted_iota(jnp.int32, (TT, TT), 1)
    before = (s_i < t_i).astype(BF16)
    pos = _dot(member.astype(BF16), before) + base_ref[:, 0:1]
    for kk in range(TOP_K):
        idx_ref[kk:kk + 1, :] = idxs[kk]
        gate_ref[kk:kk + 1, :] = ex[kk] / tot
        rank_ref[kk:kk + 1, :] = jnp.sum(jnp.where(hots[kk], pos, 0.0), axis=0,
                                         keepdims=True).astype(jnp.int32)
    base_ref[...] = base_ref[...] + jnp.sum(member, axis=1, keepdims=True)
    cnt_ref[...] = base_ref[...].astype(jnp.int32)


def _router(xt, router_w, router_b):
    T, D = xt.shape
    E = router_w.shape[1]
    assert T % ROUTER_TILE == 0
    tok = lambda i: (0, i)
    return pl.pallas_call(
        _router_kernel,
        grid=(T // ROUTER_TILE,),
        in_specs=[
            pl.BlockSpec((ROUTER_TILE, D), lambda i: (i, 0)),
            pl.BlockSpec((E, D), lambda i: (0, 0)),
            pl.BlockSpec((E, 1), lambda i: (0, 0)),
        ],
        out_specs=[
            pl.BlockSpec((TOP_K, ROUTER_TILE), tok),
            pl.BlockSpec((TOP_K, ROUTER_TILE), tok),
            pl.BlockSpec((TOP_K, ROUTER_TILE), tok),
            pl.BlockSpec((E, LANES), lambda i: (0, 0)),
        ],
        out_shape=[
            jax.ShapeDtypeStruct((TOP_K, T), jnp.int32),
            jax.ShapeDtypeStruct((TOP_K, T), F32),
            jax.ShapeDtypeStruct((TOP_K, T), jnp.int32),
            jax.ShapeDtypeStruct((E, LANES), jnp.int32),
        ],
        scratch_shapes=[pltpu.VMEM((E, LANES), F32)],
        compiler_params=pltpu.CompilerParams(
            dimension_semantics=("arbitrary",), vmem_limit_bytes=VMEM_LIMIT),
        name="router",
    )(xt, router_w.T.astype(F32), router_b.reshape(E, 1).astype(F32))


def _dispatch_kernel(dest_ref, x_ref, xs_in_ref, xs_ref, sem):
    del xs_in_ref
    TT = x_ref.shape[0]

    def row_copy(t, kk):
        d = dest_ref[0, 0, kk * TT + t]
        return pltpu.make_async_copy(x_ref.at[pl.ds(t, 1), :], xs_ref.at[pl.ds(d, 1), :], sem)

    def issue(t, carry):
        for kk in range(TOP_K):
            row_copy(t, kk).start()
        return carry

    def drain(t, carry):
        for kk in range(TOP_K):
            row_copy(t, kk).wait()
        return carry

    lax.fori_loop(0, TT, issue, 0)
    lax.fori_loop(0, TT, drain, 0)


def _dispatch(xt, dest_tiles, n_rows):
    T, D = xt.shape
    nt = T // DISPATCH_TILE
    xs0 = jnp.zeros((n_rows, D), F32)
    return pl.pallas_call(
        _dispatch_kernel,
        grid=(nt,),
        in_specs=[
            pl.BlockSpec((1, 1, TOP_K * DISPATCH_TILE), lambda i: (i, 0, 0),
                         memory_space=pltpu.SMEM),
            pl.BlockSpec((DISPATCH_TILE, D), lambda i: (i, 0)),
            pl.BlockSpec(memory_space=pl.ANY),
        ],
        out_specs=pl.BlockSpec(memory_space=pl.ANY),
        out_shape=jax.ShapeDtypeStruct((n_rows, D), F32),
        scratch_shapes=[pltpu.SemaphoreType.DMA(())],
        input_output_aliases={2: 0},
        compiler_params=pltpu.CompilerParams(
            dimension_semantics=("arbitrary",), vmem_limit_bytes=VMEM_LIMIT),
        name="dispatch",
    )(dest_tiles, xt, xs0)


def _expert_kernel(ge_ref, xs_ref, wu_ref, bu_ref, wd_ref, bd_ref, ys_ref):
    del ge_ref
    F = wd_ref.shape[0]
    hb = _dot(xs_ref[...].astype(BF16), wu_ref[...]) + bu_ref[...]
    glu = jnp.minimum(hb[:, :F], SWIGLU_LIMIT)
    lin = jnp.clip(hb[:, F:], -SWIGLU_LIMIT, SWIGLU_LIMIT)
    act = glu * jax.nn.sigmoid(SWIGLU_ALPHA * glu) * (lin + 1.0)
    ys_ref[...] = _dot(act.astype(BF16), wd_ref[...]) + bd_ref[...]


def _experts(xs, group_expert, w_up, b_up, w_down, b_down):
    P, D = xs.shape
    E, _, F2 = w_up.shape
    F = w_down.shape[1]
    nb = P // EXPERT_ROWS
    grid_spec = pltpu.PrefetchScalarGridSpec(
        num_scalar_prefetch=1,
        grid=(nb,),
        in_specs=[
            pl.BlockSpec((EXPERT_ROWS, D), lambda b, ge: (b, 0)),
            pl.BlockSpec((None, D, F2), lambda b, ge: (ge[b], 0, 0)),
            pl.BlockSpec((None, 1, F2), lambda b, ge: (ge[b], 0, 0)),
            pl.BlockSpec((None, F, D), lambda b, ge: (ge[b], 0, 0)),
            pl.BlockSpec((None, 1, D), lambda b, ge: (ge[b], 0, 0)),
        ],
        out_specs=pl.BlockSpec((EXPERT_ROWS, D), lambda b, ge: (b, 0)),
    )
    return pl.pallas_call(
        _expert_kernel,
        grid_spec=grid_spec,
        out_shape=jax.ShapeDtypeStruct((P, D), F32),
        compiler_params=pltpu.CompilerParams(
            dimension_semantics=("arbitrary",), vmem_limit_bytes=VMEM_LIMIT),
        name="experts",
    )(group_expert, xs, w_up.astype(BF16), b_up.reshape(E, 1, F2), w_down.astype(BF16),
      b_down.reshape(E, 1, D))


def _combine_kernel(dest_ref, gate_ref, x_ref, ys_ref, lng_ref, lnb_ref, out_ref, buf_ref, sem,
                    *, alpha):
    TT = x_ref.shape[0]

    def row_copy(t, kk):
        d = dest_ref[0, 0, kk * TT + t]
        return pltpu.make_async_copy(ys_ref.at[pl.ds(d, 1), :], buf_ref.at[kk, pl.ds(t, 1), :], sem)

    def issue(t, carry):
        for kk in range(TOP_K):
            row_copy(t, kk).start()
        return carry

    def drain(t, carry):
        for kk in range(TOP_K):
            row_copy(t, kk).wait()
        return carry

    lax.fori_loop(0, TT, issue, 0)
    lax.fori_loop(0, TT, drain, 0)
    f = gate_ref[:, 0:1] * buf_ref[0]
    for kk in range(1, TOP_K):
        f = f + gate_ref[:, kk:kk + 1] * buf_ref[kk]
    out_ref[...] = _layer_norm(alpha * x_ref[...] + f, lng_ref[...], lnb_ref[...])


def _combine(xt, ys, dest_tiles, gates_col, ln_g, ln_b, alpha):
    T, D = xt.shape
    nt = T // COMBINE_TILE
    kern = functools.partial(_combine_kernel, alpha=alpha)
    return pl.pallas_call(
        kern,
        grid=(nt,),
        in_specs=[
            pl.BlockSpec((1, 1, TOP_K * COMBINE_TILE), lambda i: (i, 0, 0),
                         memory_space=pltpu.SMEM),
            pl.BlockSpec((COMBINE_TILE, TOP_K), lambda i: (i, 0)),
            pl.BlockSpec((COMBINE_TILE, D), lambda i: (i, 0)),
            pl.BlockSpec(memory_space=pl.ANY),
            pl.BlockSpec((1, D), lambda i: (0, 0)),
            pl.BlockSpec((1, D), lambda i: (0, 0)),
        ],
        out_specs=pl.BlockSpec((COMBINE_TILE, D), lambda i: (i, 0)),
        out_shape=jax.ShapeDtypeStruct((T, D), F32),
        scratch_shapes=[pltpu.VMEM((TOP_K, COMBINE_TILE, D), F32), pltpu.SemaphoreType.DMA(())],
        compiler_params=pltpu.CompilerParams(
            dimension_semantics=("arbitrary",), vmem_limit_bytes=VMEM_LIMIT),
        name="combine",
    )(dest_tiles, gates_col, xt, ys, ln_g.reshape(1, D), ln_b.reshape(1, D))


def _tile_slots(dest, tile):
    K, T = dest.shape
    return dest.reshape(K, T // tile, tile).transpose(1, 0, 2).reshape(T // tile, 1, K * tile)


def _moe(xt, router_w, router_b, w_up, b_up, w_down, b_down, ln_g, ln_b, alpha):
    T, D = xt.shape
    E = router_w.shape[1]
    R = EXPERT_ROWS
    idx, gates, rank, cnt = _router(xt, router_w, router_b)
    counts = cnt[:, 0]
    padded = ((counts + R - 1) // R) * R
    pends = jnp.cumsum(padded)
    pstarts = pends - padded
    dest = pstarts[idx] + rank
    n_rows = -(-(T * TOP_K + E * (R - 1)) // R) * R
    nb = n_rows // R
    group_expert = jnp.minimum(
        jnp.searchsorted(pends, jnp.arange(nb, dtype=jnp.int32) * R, side="right"),
        E - 1).astype(jnp.int32)
    xs = _dispatch(xt, _tile_slots(dest, DISPATCH_TILE), n_rows)
    ys = _experts(xs, group_expert, w_up, b_up, w_down, b_down)
    return _combine(xt, ys, _tile_slots(dest, COMBINE_TILE), gates.T, ln_g, ln_b, alpha)


def kernel(x, ab_w_in, ab_b_igate, ab_b_fgate, ab_conv_w, ab_head_gain, ab_w_out, cf_w_pw1, cf_w_dw,
           cf_b_dw, cf_ln_g, cf_ln_b, cf_w_pw2, router_w, router_b, exp_w_up, exp_b_up, exp_w_down,
           exp_b_down, post_ln_g, post_ln_b):
    Bn, S, D = x.shape
    depth = post_ln_g.shape[0]
    alpha = float((2 * depth) ** 0.25)
    for layer in range(depth):
        j = layer // 2
        if layer % 2 == 0:
            x = _mixer_ab(x, ab_w_in[j], ab_b_igate[j], ab_b_fgate[j], ab_conv_w[j], ab_head_gain[j],
                          ab_w_out[j], post_ln_g[layer, 0], post_ln_b[layer, 0], alpha)
        else:
            x = _conformer(x, cf_w_pw1[j], cf_w_dw[j], cf_b_dw[j], cf_ln_g[j], cf_ln_b[j],
                           cf_w_pw2[j], post_ln_g[layer, 0], post_ln_b[layer, 0], alpha)
        xt = _moe(x.reshape(Bn * S, D), router_w[layer], router_b[layer], exp_w_up[layer],
                  exp_b_up[layer], exp_w_down[layer], exp_b_down[layer], post_ln_g[layer, 1],
                  post_ln_b[layer, 1], alpha)
        x = xt.reshape(Bn, S, D)
    return x
```

```python
import functools

import jax
import jax.numpy as jnp
from jax import lax
from jax.experimental import pallas as pl
from jax.experimental.pallas import tpu as pltpu

F32 = jnp.float32
BF16 = jnp.bfloat16

LANES = 128
SUBLANES = 8

MLSTM_HEADS = 4
MLSTM_QK_DIM = 64
MLSTM_V_DIM = 128
TOP_K = 4
SWIGLU_LIMIT = 7.0
SWIGLU_ALPHA = 1.702
LN_EPS = 1e-5
HEAD_NORM_EPS = 1e-6

SEQ_TILE = 512
MLSTM_CHUNK = 128
ROUTER_TILE = 512
DISPATCH_TILE = 256
COMBINE_TILE = 256
EXPERT_ROWS = 256
CONV_ROWS = 64
CAST_ROWS = 128
VMEM_LIMIT = 56 * 1024 * 1024


def _layer_norm(y, g, b):
    mu = jnp.mean(y, axis=-1, keepdims=True)
    yc = y - mu
    var = jnp.mean(yc * yc, axis=-1, keepdims=True)
    return yc * lax.rsqrt(var + LN_EPS) * g + b


def _log_sigmoid(x):
    return jnp.minimum(x, 0.0) - jnp.log1p(jnp.exp(-jnp.abs(x)))


def _dot(a, b):
    return jnp.dot(a, b, preferred_element_type=F32)


def _dot_nt(a, b):
    return lax.dot_general(a, b, (((1,), (1,)), ((), ())), preferred_element_type=F32)


def _dot_tn(a, b):
    return lax.dot_general(a, b, (((0,), (0,)), ((), ())), preferred_element_type=F32)


def _dot_f32(a, b):
    return jnp.dot(a, b, preferred_element_type=F32, precision=lax.Precision.HIGHEST)


def _mixer_ab_kernel(x_ref, wall_ref, wift_ref, bcol_ref, brow_ref, convw_ref, gain_ref, wout_ref,
                     lng_ref, lnb_ref, out_ref,
                     z_ref, hy_ref, ubuf_ref, c_ref, n_ref, m_ref, *, alpha, sconv_w):
    H, L, TS = MLSTM_HEADS, MLSTM_CHUNK, SEQ_TILE
    HW = H * LANES
    q0, k0, v0, o0 = 0, HW, 2 * HW, 3 * HW
    sb0 = 4 * HW
    sc0 = sb0 + sconv_w
    sx0 = sc0 + sconv_w
    if0 = sx0 + sconv_w

    @pl.when(pl.program_id(1) == 0)
    def _():
        c_ref[...] = jnp.zeros_like(c_ref)
        n_ref[...] = jnp.zeros_like(n_ref)
        m_ref[...] = jnp.zeros_like(m_ref)
        ubuf_ref[0:SUBLANES, :] = jnp.zeros((SUBLANES, sconv_w), F32)

    x = x_ref[...]
    xb = x.astype(BF16)
    z_ref[...] = _dot(xb, wall_ref[...])
    pre_row = _dot_nt(wift_ref[...], xb) + brow_ref[...]
    logf_row = _log_sigmoid(pre_row)
    pre_col = z_ref[:, if0:if0 + LANES] + bcol_ref[...]
    logf_col = _log_sigmoid(pre_col)

    r_i = lax.broadcasted_iota(jnp.int32, (L, L), 0)
    c_i = lax.broadcasted_iota(jnp.int32, (L, L), 1)
    causal = r_i >= c_i
    tril = causal.astype(F32)
    triu = (r_i <= c_i).astype(F32)

    for c in range(TS // L):
        r0 = c * L
        b_col = _dot_f32(tril, logf_col[r0:r0 + L, :])
        b_row = _dot_f32(logf_row[:, r0:r0 + L], triu)
        for h in range(H):
            q = z_ref[r0:r0 + L, q0 + h * LANES:q0 + (h + 1) * LANES]
            k = z_ref[r0:r0 + L, k0 + h * LANES:k0 + (h + 1) * LANES]
            v = z_ref[r0:r0 + L, v0 + h * LANES:v0 + (h + 1) * LANES]
            o = z_ref[r0:r0 + L, o0 + h * LANES:o0 + (h + 1) * LANES]
            qb, kb, vb = q.astype(BF16), k.astype(BF16), v.astype(BF16)
            bc = b_col[:, H + h:H + h + 1]
            ic = pre_col[r0:r0 + L, h:h + 1]
            br = b_row[H + h:H + h + 1, :]
            ir = pre_row[h:h + 1, r0:r0 + L]
            c_prev = c_ref[h]
            n_prev = n_ref[h]
            m_prev = m_ref[h][:, 0:1]

            dmat = jnp.where(causal, bc - br + ir, -jnp.inf)
            e_inter = bc + m_prev
            m_t = jnp.maximum(e_inter, jnp.max(dmat, axis=-1, keepdims=True))
            w_intra = jnp.exp(dmat - m_t)
            w_inter = jnp.exp(e_inter - m_t)
            s = _dot_nt(qb, kb) * w_intra
            num = w_inter * _dot(qb, c_prev.astype(BF16)) + _dot(s.astype(BF16), vb)
            den = (w_inter * jnp.sum(q * n_prev, axis=-1, keepdims=True)
                   + jnp.sum(s, axis=-1, keepdims=True))
            hh = num / jnp.maximum(jnp.abs(den), jnp.exp(-m_t))

            g = bc[L - 1:L, :]
            a_col = g - bc + ic
            m_new = jnp.maximum(g + m_prev, jnp.max(a_col, axis=0, keepdims=True))
            decay = jnp.exp(g + m_prev - m_new)
            kw = k * jnp.exp(a_col - m_new)
            c_ref[h] = decay * c_prev + _dot_tn(kw.astype(BF16), vb)
            n_ref[h] = decay * n_prev + jnp.sum(kw, axis=0, keepdims=True)
            m_ref[h] = jnp.broadcast_to(m_new, (1, LANES))

            mu = jnp.mean(hh, axis=-1, keepdims=True)
            hc = hh - mu
            var = jnp.mean(hc * hc, axis=-1, keepdims=True)
            hn = hc * lax.rsqrt(var + HEAD_NORM_EPS) * gain_ref[:, h * LANES:(h + 1) * LANES]
            hy_ref[r0:r0 + L, h * LANES:(h + 1) * LANES] = (jax.nn.sigmoid(o) * hn).astype(BF16)

    u = z_ref[:, sc0:sc0 + sconv_w] * z_ref[:, sx0:sx0 + sconv_w]
    ubuf_ref[SUBLANES:SUBLANES + TS, :] = u
    conv = (convw_ref[2:3, :] * u
            + convw_ref[1:2, :] * ubuf_ref[SUBLANES - 1:SUBLANES - 1 + TS, :]
            + convw_ref[0:1, :] * ubuf_ref[SUBLANES - 2:SUBLANES - 2 + TS, :])
    hy_ref[:, HW:HW + sconv_w] = (z_ref[:, sb0:sb0 + sconv_w] * conv).astype(BF16)
    ubuf_ref[0:SUBLANES, :] = ubuf_ref[TS:TS + SUBLANES, :]

    mix = _dot(hy_ref[...], wout_ref[...])
    out_ref[...] = _layer_norm(alpha * x + mix, lng_ref[...], lnb_ref[...])


def _mixer_ab(x, w_in, b_igate, b_fgate, conv_w, head_gain, w_out, ln_g, ln_b, alpha):
    Bn, S, D = x.shape
    H, dk, dv = MLSTM_HEADS, MLSTM_QK_DIM, MLSTM_V_DIM
    assert dv == LANES and dk <= LANES and S % SEQ_TILE == 0 and SEQ_TILE % MLSTM_CHUNK == 0
    hw = H * dv
    sw = D - hw
    cuts = [H * dk, 2 * H * dk, 2 * H * dk + hw, 2 * H * dk + 2 * hw]
    wq, wk, wv, wo = (w_in[:, a:b] for a, b in zip([0] + cuts[:-1], cuts))
    wi = w_in[:, cuts[-1]:cuts[-1] + H]
    wf = w_in[:, cuts[-1] + H:cuts[-1] + 2 * H]
    ws = w_in[:, cuts[-1] + 2 * H:]

    def pad_heads(w, scale):
        w = (w * scale).reshape(D, H, dk)
        return jnp.pad(w, ((0, 0), (0, 0), (0, LANES - dk))).reshape(D, H * LANES)

    w_if = jnp.concatenate([wi, wf], axis=1)
    w_all = jnp.concatenate(
        [pad_heads(wq, 1.0), pad_heads(wk, dk ** -0.5), wv, wo, ws,
         jnp.pad(w_if, ((0, 0), (0, LANES - 2 * H)))], axis=1).astype(BF16)
    nz = w_all.shape[1]
    w_if_t = w_if.T.astype(BF16)
    b_if = jnp.concatenate([b_igate, b_fgate]).astype(F32)
    b_col = jnp.pad(b_if, (0, LANES - 2 * H)).reshape(1, LANES)
    b_row = b_if.reshape(2 * H, 1)

    const = lambda b, j: (0, 0)
    kern = functools.partial(_mixer_ab_kernel, alpha=alpha, sconv_w=sw)
    return pl.pallas_call(
        kern,
        grid=(Bn, S // SEQ_TILE),
        in_specs=[
            pl.BlockSpec((None, SEQ_TILE, D), lambda b, j: (b, j, 0)),
            pl.BlockSpec((D, nz), const),
            pl.BlockSpec((2 * H, D), const),
            pl.BlockSpec((1, LANES), const),
            pl.BlockSpec((2 * H, 1), const),
            pl.BlockSpec((conv_w.shape[0], sw), const),
            pl.BlockSpec((1, hw), const),
            pl.BlockSpec((D, D), const),
            pl.BlockSpec((1, D), const),
            pl.BlockSpec((1, D), const),
        ],
        out_specs=pl.BlockSpec((None, SEQ_TILE, D), lambda b, j: (b, j, 0)),
        out_shape=jax.ShapeDtypeStruct((Bn, S, D), F32),
        scratch_shapes=[
            pltpu.VMEM((SEQ_TILE, nz), F32),
            pltpu.VMEM((SEQ_TILE, D), BF16),
            pltpu.VMEM((SEQ_TILE + 2 * SUBLANES, sw), F32),
            pltpu.VMEM((H, LANES, dv), F32),
            pltpu.VMEM((H, 1, LANES), F32),
            pltpu.VMEM((H, 1, LANES), F32),
        ],
        compiler_params=pltpu.CompilerParams(
            dimension_semantics=("parallel", "arbitrary"), vmem_limit_bytes=VMEM_LIMIT),
        name="mixer_ab",
    )(x, w_all, w_if_t, b_col, b_row, conv_w.astype(F32), head_gain.reshape(1, hw).astype(F32),
      w_out.astype(BF16), ln_g.reshape(1, D), ln_b.reshape(1, D))


def _conformer_kernel(x_ref, w1_ref, wdw_ref, bdw_ref, cg_ref, cb_ref, w2_ref, lng_ref, lnb_ref,
                      out_ref, ubuf_ref, uc_ref, sh_ref, *, alpha, taps, hist):
    TS = SEQ_TILE
    C = uc_ref.shape[1]

    @pl.when(pl.program_id(1) == 0)
    def _():
        ubuf_ref[0:hist, :] = jnp.zeros((hist, C), F32)

    x = x_ref[...]
    z = _dot(x.astype(BF16), w1_ref[...])
    ubuf_ref[hist:hist + TS, :] = z[:, :C] * jax.nn.sigmoid(z[:, C:])

    base = hist - (taps - 1)
    sh_rows = sh_ref.shape[1]
    for cb in range(C // LANES):
        cs = slice(cb * LANES, (cb + 1) * LANES)
        for r in range(1, SUBLANES):
            sh_ref[r - 1] = ubuf_ref[r:r + sh_rows, cs]
        for rb in range(TS // CONV_ROWS):
            acc = jnp.zeros((CONV_ROWS, LANES), F32) + bdw_ref[:, cs]
            for j in range(taps):
                r = (base + j) % SUBLANES
                start = rb * CONV_ROWS + base + j - r
                if r == 0:
                    win = ubuf_ref[start:start + CONV_ROWS, cs]
                else:
                    win = sh_ref[r - 1, start:start + CONV_ROWS, :]
                acc = acc + wdw_ref[j:j + 1, cs] * win
            uc_ref[rb * CONV_ROWS:(rb + 1) * CONV_ROWS, cs] = acc
    ubuf_ref[0:hist, :] = ubuf_ref[TS:TS + hist, :]

    un = _layer_norm(uc_ref[...], cg_ref[...], cb_ref[...])
    act = un * jax.nn.sigmoid(un)
    mix = _dot(act.astype(BF16), w2_ref[...])
    out_ref[...] = _layer_norm(alpha * x + mix, lng_ref[...], lnb_ref[...])


def _conformer(x, w_pw1, w_dw, b_dw, cg, cb, w_pw2, ln_g, ln_b, alpha):
    Bn, S, D = x.shape
    taps, C = w_dw.shape
    hist = -(-(taps - 1) // SUBLANES) * SUBLANES
    assert S % SEQ_TILE == 0 and C % LANES == 0 and hist <= SEQ_TILE
    const = lambda b, j: (0, 0)
    kern = functools.partial(_conformer_kernel, alpha=alpha, taps=taps, hist=hist)
    return pl.pallas_call(
        kern,
        grid=(Bn, S // SEQ_TILE),
        in_specs=[
            pl.BlockSpec((None, SEQ_TILE, D), lambda b, j: (b, j, 0)),
            pl.BlockSpec((D, 2 * C), const),
            pl.BlockSpec((taps, C), const),
            pl.BlockSpec((1, C), const),
            pl.BlockSpec((1, C), const),
            pl.BlockSpec((1, C), const),
            pl.BlockSpec((C, D), const),
            pl.BlockSpec((1, D), const),
            pl.BlockSpec((1, D), const),
        ],
        out_specs=pl.BlockSpec((None, SEQ_TILE, D), lambda b, j: (b, j, 0)),
        out_shape=jax.ShapeDtypeStruct((Bn, S, D), F32),
        scratch_shapes=[
            pltpu.VMEM((SEQ_TILE + hist, C), F32),
            pltpu.VMEM((SEQ_TILE, C), F32),
            pltpu.VMEM((SUBLANES - 1, SEQ_TILE + hist - SUBLANES, LANES), F32),
        ],
        compiler_params=pltpu.CompilerParams(
            dimension_semantics=("parallel", "arbitrary"), vmem_limit_bytes=VMEM_LIMIT),
        name="conformer",
    )(x, w_pw1.astype(BF16), w_dw.astype(F32), b_dw.reshape(1, C), cg.reshape(1, C),
      cb.reshape(1, C), w_pw2.astype(BF16), ln_g.reshape(1, D), ln_b.reshape(1, D))


def _router_kernel(x_ref, rwt_ref, rb_ref, idx_ref, gate_ref, rank_ref, cnt_ref, base_ref):
    E = rwt_ref.shape[0]
    TT = x_ref.shape[0]

    @pl.when(pl.program_id(0) == 0)
    def _():
        base_ref[...] = jnp.zeros_like(base_ref)

    logits = lax.dot_general(rwt_ref[...], x_ref[...], (((1,), (1,)), ((), ())),
                             preferred_element_type=F32,
                             precision=lax.Precision.HIGHEST) + rb_ref[...]
    e_iota = lax.broadcasted_iota(jnp.int32, (E, TT), 0)
    work = logits
    vals, idxs, hots = [], [], []
    for _ in range(TOP_K):
        mx = jnp.max(work, axis=0, keepdims=True)
        ix = jnp.min(jnp.where(work == mx, e_iota, E), axis=0, keepdims=True)
        hot = e_iota == ix
        work = jnp.where(hot, -jnp.inf, work)
        vals.append(mx)
        idxs.append(ix)
        hots.append(hot)
    ex = [jnp.exp(v - vals[0]) for v in vals]
    tot = functools.reduce(lambda a, b: a + b, ex)

    member = functools.reduce(lambda a, b: a | b, hots).astype(F32)
    s_i = lax.broadcasted_iota(jnp.int32, (TT, TT), 0)
    t_i = lax.broadcasted_iota(jnp.int32, (TT, TT), 1)
    before = (s_i < t_i).astype(BF16)
    pos = _dot(member.astype(BF16), before) + base_ref[:, 0:1]
    for kk in range(TOP_K):
        idx_ref[kk:kk + 1, :] = idxs[kk]
        gate_ref[kk:kk + 1, :] = ex[kk] / tot
        rank_ref[kk:kk + 1, :] = jnp.sum(jnp.where(hots[kk], pos, 0.0), axis=0,
                                         keepdims=True).astype(jnp.int32)
    base_ref[...] = base_ref[...] + jnp.sum(member, axis=1, keepdims=True)
    cnt_ref[...] = base_ref[...].astype(jnp.int32)


def _router(xt, router_w, router_b):
    T, D = xt.shape
    E = router_w.shape[1]
    assert T % ROUTER_TILE == 0
    tok = lambda i: (0, i)
    return pl.pallas_call(
        _router_kernel,
        grid=(T // ROUTER_TILE,),
        in_specs=[
            pl.BlockSpec((ROUTER_TILE, D), lambda i: (i, 0)),
            pl.BlockSpec((E, D), lambda i: (0, 0)),
            pl.BlockSpec((E, 1), lambda i: (0, 0)),
        ],
        out_specs=[
            pl.BlockSpec((TOP_K, ROUTER_TILE), tok),
            pl.BlockSpec((TOP_K, ROUTER_TILE), tok),
            pl.BlockSpec((TOP_K, ROUTER_TILE), tok),
            pl.BlockSpec((E, LANES), lambda i: (0, 0)),
        ],
        out_shape=[
            jax.ShapeDtypeStruct((TOP_K, T), jnp.int32),
            jax.ShapeDtypeStruct((TOP_K, T), F32),
            jax.ShapeDtypeStruct((TOP_K, T), jnp.int32),
            jax.ShapeDtypeStruct((E, LANES), jnp.int32),
        ],
        scratch_shapes=[pltpu.VMEM((E, LANES), F32)],
        compiler_params=pltpu.CompilerParams(
            dimension_semantics=("arbitrary",), vmem_limit_bytes=VMEM_LIMIT),
        name="router",
    )(xt, router_w.T.astype(F32), router_b.reshape(E, 1).astype(F32))


def _dispatch_kernel(pad_start_ref, pad_cnt_ref, nbu_ref, dest_ref, x_ref, xs_ref, zblk_ref, sem):
    TT = x_ref.shape[0]
    E = pad_start_ref.shape[0]
    R = zblk_ref.shape[0]
    nb = xs_ref.shape[0] // R

    def row_copy(t, kk):
        d = dest_ref[0, 0, kk * TT + t]
        return pltpu.make_async_copy(x_ref.at[pl.ds(t, 1), :], xs_ref.at[pl.ds(d, 1), :], sem)

    def issue(t, carry):
        for kk in range(TOP_K):
            row_copy(t, kk).start()
        return carry

    def drain(t, carry):
        for kk in range(TOP_K):
            row_copy(t, kk).wait()
        return carry

    lax.fori_loop(0, TT, issue, 0)
    lax.fori_loop(0, TT, drain, 0)

    @pl.when(pl.program_id(0) == pl.num_programs(0) - 1)
    def _():
        zblk_ref[...] = jnp.zeros_like(zblk_ref)

        def pad_copy(e, r):
            return pltpu.make_async_copy(zblk_ref.at[pl.ds(0, 1), :],
                                         xs_ref.at[pl.ds(pad_start_ref[e] + r, 1), :], sem)

        def tail_copy(blk):
            return pltpu.make_async_copy(zblk_ref, xs_ref.at[pl.ds(blk * R, R), :], sem)

        def for_each_pad(fn):
            def per_expert(e, carry):
                def per_row(r, c):
                    fn(pad_copy(e, r))
                    return c
                lax.fori_loop(0, pad_cnt_ref[e], per_row, 0)
                return carry
            lax.fori_loop(0, E, per_expert, 0)

            def per_block(blk, carry):
                fn(tail_copy(blk))
                return carry
            lax.fori_loop(nbu_ref[0], nb, per_block, 0)

        for_each_pad(lambda cp: cp.start())
        for_each_pad(lambda cp: cp.wait())


def _dispatch(xt, dest_tiles, pad_start, pad_cnt, nb_used, n_rows):
    T, D = xt.shape
    nt = T // DISPATCH_TILE
    grid_spec = pltpu.PrefetchScalarGridSpec(
        num_scalar_prefetch=3,
        grid=(nt,),
        in_specs=[
            pl.BlockSpec((1, 1, TOP_K * DISPATCH_TILE), lambda i, ps, pc, nbu: (i, 0, 0),
                         memory_space=pltpu.SMEM),
            pl.BlockSpec((DISPATCH_TILE, D), lambda i, ps, pc, nbu: (i, 0)),
        ],
        out_specs=pl.BlockSpec(memory_space=pl.ANY),
        scratch_shapes=[pltpu.VMEM((EXPERT_ROWS, D), F32), pltpu.SemaphoreType.DMA(())],
    )
    return pl.pallas_call(
        _dispatch_kernel,
        grid_spec=grid_spec,
        out_shape=jax.ShapeDtypeStruct((n_rows, D), F32),
        compiler_params=pltpu.CompilerParams(
            dimension_semantics=("arbitrary",), vmem_limit_bytes=VMEM_LIMIT),
        name="dispatch",
    )(pad_start, pad_cnt, nb_used, dest_tiles, xt)


def _cast_rows(src_ref, dst_ref):
    rows = src_ref.shape[0]

    def body(i, carry):
        r = pl.multiple_of(i * CAST_ROWS, CAST_ROWS)
        dst_ref[pl.ds(r, CAST_ROWS), :] = src_ref[pl.ds(r, CAST_ROWS), :].astype(dst_ref.dtype)
        return carry

    lax.fori_loop(0, rows // CAST_ROWS, body, 0)


def _expert_kernel(ge_ref, nbu_ref, xs_ref, wu_ref, bu_ref, wd_ref, bd_ref, ys_ref, wub_ref, wdb_ref):
    F = wd_ref.shape[0]
    b = pl.program_id(0)

    @pl.when(b >= nbu_ref[0])
    def _():
        ys_ref[...] = jnp.zeros_like(ys_ref)

    @pl.when(b < nbu_ref[0])
    def _():
        @pl.when((b == 0) | (ge_ref[b] != ge_ref[jnp.maximum(b - 1, 0)]))
        def _():
            _cast_rows(wu_ref, wub_ref)
            _cast_rows(wd_ref, wdb_ref)

        hb = _dot(xs_ref[...].astype(BF16), wub_ref[...]) + bu_ref[...]
        glu = jnp.minimum(hb[:, :F], SWIGLU_LIMIT)
        lin = jnp.clip(hb[:, F:], -SWIGLU_LIMIT, SWIGLU_LIMIT)
        act = glu * jax.nn.sigmoid(SWIGLU_ALPHA * glu) * (lin + 1.0)
        ys_ref[...] = _dot(act.astype(BF16), wdb_ref[...]) + bd_ref[...]


def _experts(xs, group_expert, nb_used, layer, w_up, b_up, w_down, b_down):
    P, D = xs.shape
    depth, E, _, F2 = w_up.shape
    F = w_down.shape[2]
    nb = P // EXPERT_ROWS
    assert D % CAST_ROWS == 0 and F % CAST_ROWS == 0
    row_block = lambda b, ge, nbu: (jnp.minimum(b, nbu[0] - 1), 0)
    expert_block = lambda b, ge, nbu: (layer, ge[b], 0, 0)
    grid_spec = pltpu.PrefetchScalarGridSpec(
        num_scalar_prefetch=2,
        grid=(nb,),
        in_specs=[
            pl.BlockSpec((EXPERT_ROWS, D), row_block),
            pl.BlockSpec((None, None, D, F2), expert_block),
            pl.BlockSpec((None, None, 1, F2), expert_block),
            pl.BlockSpec((None, None, F, D), expert_block),
            pl.BlockSpec((None, None, 1, D), expert_block),
        ],
        out_specs=pl.BlockSpec((EXPERT_ROWS, D), lambda b, ge, nbu: (b, 0)),
        scratch_shapes=[pltpu.VMEM((D, F2), BF16), pltpu.VMEM((F, D), BF16)],
    )
    return pl.pallas_call(
        _expert_kernel,
        grid_spec=grid_spec,
        out_shape=jax.ShapeDtypeStruct((P, D), F32),
        compiler_params=pltpu.CompilerParams(
            dimension_semantics=("arbitrary",), vmem_limit_bytes=VMEM_LIMIT),
        name="experts",
    )(group_expert, nb_used, xs, w_up, b_up.reshape(depth, E, 1, F2), w_down,
      b_down.reshape(depth, E, 1, D))


def _combine_kernel(dest_ref, dnext_ref, gate_ref, x_ref, ys_ref, lng_ref, lnb_ref, out_ref,
                    buf_ref, sems, *, alpha):
    TT = x_ref.shape[0]
    i = pl.program_id(0)
    slot = lax.rem(i, 2)

    def for_each_row(idx_ref, s, fn):
        def body(t, carry):
            for kk in range(TOP_K):
                d = idx_ref[0, 0, kk * TT + t]
                fn(pltpu.make_async_copy(ys_ref.at[pl.ds(d, 1), :],
                                         buf_ref.at[s, kk, pl.ds(t, 1), :], sems.at[s]))
            return carry
        lax.fori_loop(0, TT, body, 0)

    @pl.when(i == 0)
    def _():
        for_each_row(dest_ref, 0, lambda cp: cp.start())

    @pl.when(i + 1 < pl.num_programs(0))
    def _():
        for_each_row(dnext_ref, 1 - slot, lambda cp: cp.start())

    for_each_row(dest_ref, slot, lambda cp: cp.wait())
    f = gate_ref[:, 0:1] * buf_ref[slot, 0]
    for kk in range(1, TOP_K):
        f = f + gate_ref[:, kk:kk + 1] * buf_ref[slot, kk]
    out_ref[...] = _layer_norm(alpha * x_ref[...] + f, lng_ref[...], lnb_ref[...])


def _combine(xt, ys, dest_tiles, gates_col, ln_g, ln_b, alpha):
    T, D = xt.shape
    nt = T // COMBINE_TILE
    kern = functools.partial(_combine_kernel, alpha=alpha)
    slots = (1, 1, TOP_K * COMBINE_TILE)
    return pl.pallas_call(
        kern,
        grid=(nt,),
        in_specs=[
            pl.BlockSpec(slots, lambda i: (i, 0, 0), memory_space=pltpu.SMEM),
            pl.BlockSpec(slots, lambda i: (jnp.minimum(i + 1, nt - 1), 0, 0),
                         memory_space=pltpu.SMEM),
            pl.BlockSpec((COMBINE_TILE, TOP_K), lambda i: (i, 0)),
            pl.BlockSpec((COMBINE_TILE, D), lambda i: (i, 0)),
            pl.BlockSpec(memory_space=pl.ANY),
            pl.BlockSpec((1, D), lambda i: (0, 0)),
            pl.BlockSpec((1, D), lambda i: (0, 0)),
        ],
        out_specs=pl.BlockSpec((COMBINE_TILE, D), lambda i: (i, 0)),
        out_shape=jax.ShapeDtypeStruct((T, D), F32),
        scratch_shapes=[pltpu.VMEM((2, TOP_K, COMBINE_TILE, D), F32),
                        pltpu.SemaphoreType.DMA((2,))],
        compiler_params=pltpu.CompilerParams(
            dimension_semantics=("arbitrary",), vmem_limit_bytes=VMEM_LIMIT),
        name="combine",
    )(dest_tiles, dest_tiles, gates_col, xt, ys, ln_g.reshape(1, D), ln_b.reshape(1, D))


def _tile_slots(dest, tile):
    K, T = dest.shape
    return dest.reshape(K, T // tile, tile).transpose(1, 0, 2).reshape(T // tile, 1, K * tile)


def _moe(xt, layer, router_w, router_b, w_up, b_up, w_down, b_down, ln_g, ln_b, alpha):
    T, D = xt.shape
    E = router_w.shape[1]
    R = EXPERT_ROWS
    idx, gates, rank, cnt = _router(xt, router_w, router_b)
    counts = cnt[:, 0]
    padded = ((counts + R - 1) // R) * R
    pends = jnp.cumsum(padded)
    pstarts = pends - padded
    e_ids = jnp.arange(E, dtype=jnp.int32)[:, None, None]
    dest = rank + jnp.sum(jnp.where(idx[None] == e_ids, pstarts[:, None, None], 0), axis=0)
    n_rows = -(-(T * TOP_K + E * (R - 1)) // R) * R
    nb = n_rows // R
    nb_used = (pends[E - 1:] // R).astype(jnp.int32)
    blk = jnp.minimum(jnp.arange(nb, dtype=jnp.int32), nb_used - 1) * R
    group_expert = jnp.sum((pends[None, :] <= blk[:, None]).astype(jnp.int32), axis=1)
    xs = _dispatch(xt, _tile_slots(dest, DISPATCH_TILE), (pstarts + counts).astype(jnp.int32),
                   (padded - counts).astype(jnp.int32), nb_used, n_rows)
    ys = _experts(xs, group_expert, nb_used, layer, w_up, b_up, w_down, b_down)
    return _combine(xt, ys, _tile_slots(dest, COMBINE_TILE), gates.T, ln_g, ln_b, alpha)


def kernel(x, ab_w_in, ab_b_igate, ab_b_fgate, ab_conv_w, ab_head_gain, ab_w_out, cf_w_pw1, cf_w_dw,
           cf_b_dw, cf_ln_g, cf_ln_b, cf_w_pw2, router_w, router_b, exp_w_up, exp_b_up, exp_w_down,
           exp_b_down, post_ln_g, post_ln_b):
    Bn, S, D = x.shape
    depth = post_ln_g.shape[0]
    alpha = float((2 * depth) ** 0.25)
    for layer in range(depth):
        j = layer // 2
        if layer % 2 == 0:
            x = _mixer_ab(x, ab_w_in[j], ab_b_igate[j], ab_b_fgate[j], ab_conv_w[j], ab_head_gain[j],
                          ab_w_out[j], post_ln_g[layer, 0], post_ln_b[layer, 0], alpha)
        else:
            x = _conformer(x, cf_w_pw1[j], cf_w_dw[j], cf_b_dw[j], cf_ln_g[j], cf_ln_b[j],
                           cf_w_pw2[j], post_ln_g[layer, 0], post_ln_b[layer, 0], alpha)
        xt = _moe(x.reshape(Bn * S, D), layer, router_w[layer], router_b[layer], exp_w_up,
                  exp_b_up, exp_w_down, exp_b_down, post_ln_g[layer, 1], post_ln_b[layer, 1], alpha)
        x = xt.reshape(Bn, S, D)
    return x
```

```python
import functools

import jax
import jax.numpy as jnp
from jax import lax
from jax.experimental import pallas as pl
from jax.experimental.pallas import tpu as pltpu

F32 = jnp.float32
BF16 = jnp.bfloat16

LANES = 128
SUBLANES = 8

MLSTM_HEADS = 4
MLSTM_QK_DIM = 64
MLSTM_V_DIM = 128
TOP_K = 4
SWIGLU_LIMIT = 7.0
SWIGLU_ALPHA = 1.702
LN_EPS = 1e-5
HEAD_NORM_EPS = 1e-6

SEQ_TILE = 512
MLSTM_CHUNK = 128
ROUTER_TILE = 512
DISPATCH_TILE = 256
COMBINE_TILE = 256
EXPERT_ROWS = 256
CONV_ROWS = 64
CAST_ROWS = 128
DMA_QUEUES = 2
VMEM_LIMIT = 56 * 1024 * 1024


def _layer_norm(y, g, b):
    mu = jnp.mean(y, axis=-1, keepdims=True)
    yc = y - mu
    var = jnp.mean(yc * yc, axis=-1, keepdims=True)
    return yc * lax.rsqrt(var + LN_EPS) * g + b


def _log_sigmoid(x):
    return jnp.minimum(x, 0.0) - jnp.log1p(jnp.exp(-jnp.abs(x)))


def _dot(a, b):
    return jnp.dot(a, b, preferred_element_type=F32)


def _dot_nt(a, b):
    return lax.dot_general(a, b, (((1,), (1,)), ((), ())), preferred_element_type=F32)


def _dot_tn(a, b):
    return lax.dot_general(a, b, (((0,), (0,)), ((), ())), preferred_element_type=F32)


def _dot_f32(a, b):
    return jnp.dot(a, b, preferred_element_type=F32, precision=lax.Precision.HIGHEST)


def _mixer_ab_kernel(x_ref, wall_ref, wift_ref, bcol_ref, brow_ref, convw_ref, gain_ref, wout_ref,
                     lng_ref, lnb_ref, out_ref,
                     z_ref, hy_ref, ubuf_ref, c_ref, n_ref, m_ref, *, alpha, sconv_w):
    H, L, TS = MLSTM_HEADS, MLSTM_CHUNK, SEQ_TILE
    HW = H * LANES
    q0, k0, v0, o0 = 0, HW, 2 * HW, 3 * HW
    sb0 = 4 * HW
    sc0 = sb0 + sconv_w
    sx0 = sc0 + sconv_w
    if0 = sx0 + sconv_w

    @pl.when(pl.program_id(1) == 0)
    def _():
        c_ref[...] = jnp.zeros_like(c_ref)
        n_ref[...] = jnp.zeros_like(n_ref)
        m_ref[...] = jnp.zeros_like(m_ref)
        ubuf_ref[0:SUBLANES, :] = jnp.zeros((SUBLANES, sconv_w), F32)

    x = x_ref[...]
    xb = x.astype(BF16)
    z_ref[...] = _dot(xb, wall_ref[...])
    pre_row = _dot_nt(wift_ref[...], xb) + brow_ref[...]
    logf_row = _log_sigmoid(pre_row)
    pre_col = z_ref[:, if0:if0 + LANES] + bcol_ref[...]
    logf_col = _log_sigmoid(pre_col)

    r_i = lax.broadcasted_iota(jnp.int32, (L, L), 0)
    c_i = lax.broadcasted_iota(jnp.int32, (L, L), 1)
    causal = r_i >= c_i
    tril = causal.astype(F32)
    triu = (r_i <= c_i).astype(F32)

    for c in range(TS // L):
        r0 = c * L
        b_col = _dot_f32(tril, logf_col[r0:r0 + L, :])
        b_row = _dot_f32(logf_row[:, r0:r0 + L], triu)
        for h in range(H):
            q = z_ref[r0:r0 + L, q0 + h * LANES:q0 + (h + 1) * LANES]
            k = z_ref[r0:r0 + L, k0 + h * LANES:k0 + (h + 1) * LANES]
            v = z_ref[r0:r0 + L, v0 + h * LANES:v0 + (h + 1) * LANES]
            o = z_ref[r0:r0 + L, o0 + h * LANES:o0 + (h + 1) * LANES]
            qb, kb, vb = q.astype(BF16), k.astype(BF16), v.astype(BF16)
            bc = b_col[:, H + h:H + h + 1]
            ic = pre_col[r0:r0 + L, h:h + 1]
            br = b_row[H + h:H + h + 1, :]
            ir = pre_row[h:h + 1, r0:r0 + L]
            c_prev = c_ref[h]
            n_prev = n_ref[h]
            m_prev = m_ref[h][:, 0:1]

            dmat = jnp.where(causal, bc - br + ir, -jnp.inf)
            e_inter = bc + m_prev
            m_t = jnp.maximum(e_inter, jnp.max(dmat, axis=-1, keepdims=True))
            w_intra = jnp.exp(dmat - m_t)
            w_inter = jnp.exp(e_inter - m_t)
            s = _dot_nt(qb, kb) * w_intra
            num = w_inter * _dot(qb, c_prev.astype(BF16)) + _dot(s.astype(BF16), vb)
            den = (w_inter * jnp.sum(q * n_prev, axis=-1, keepdims=True)
                   + jnp.sum(s, axis=-1, keepdims=True))
            hh = num / jnp.maximum(jnp.abs(den), jnp.exp(-m_t))

            g = bc[L - 1:L, :]
            a_col = g - bc + ic
            m_new = jnp.maximum(g + m_prev, jnp.max(a_col, axis=0, keepdims=True))
            decay = jnp.exp(g + m_prev - m_new)
            kw = k * jnp.exp(a_col - m_new)
            c_ref[h] = decay * c_prev + _dot_tn(kw.astype(BF16), vb)
            n_ref[h] = decay * n_prev + jnp.sum(kw, axis=0, keepdims=True)
            m_ref[h] = jnp.broadcast_to(m_new, (1, LANES))

            mu = jnp.mean(hh, axis=-1, keepdims=True)
            hc = hh - mu
            var = jnp.mean(hc * hc, axis=-1, keepdims=True)
            hn = hc * lax.rsqrt(var + HEAD_NORM_EPS) * gain_ref[:, h * LANES:(h + 1) * LANES]
            hy_ref[r0:r0 + L, h * LANES:(h + 1) * LANES] = (jax.nn.sigmoid(o) * hn).astype(BF16)

    u = z_ref[:, sc0:sc0 + sconv_w] * z_ref[:, sx0:sx0 + sconv_w]
    ubuf_ref[SUBLANES:SUBLANES + TS, :] = u
    conv = (convw_ref[2:3, :] * u
            + convw_ref[1:2, :] * ubuf_ref[SUBLANES - 1:SUBLANES - 1 + TS, :]
            + convw_ref[0:1, :] * ubuf_ref[SUBLANES - 2:SUBLANES - 2 + TS, :])
    hy_ref[:, HW:HW + sconv_w] = (z_ref[:, sb0:sb0 + sconv_w] * conv).astype(BF16)
    ubuf_ref[0:SUBLANES, :] = ubuf_ref[TS:TS + SUBLANES, :]

    mix = _dot(hy_ref[...], wout_ref[...])
    out_ref[...] = _layer_norm(alpha * x + mix, lng_ref[...], lnb_ref[...])


def _mixer_ab(x, w_in, b_igate, b_fgate, conv_w, head_gain, w_out, ln_g, ln_b, alpha):
    Bn, S, D = x.shape
    H, dk, dv = MLSTM_HEADS, MLSTM_QK_DIM, MLSTM_V_DIM
    assert dv == LANES and dk <= LANES and S % SEQ_TILE == 0 and SEQ_TILE % MLSTM_CHUNK == 0
    hw = H * dv
    sw = D - hw
    cuts = [H * dk, 2 * H * dk, 2 * H * dk + hw, 2 * H * dk + 2 * hw]
    wq, wk, wv, wo = (w_in[:, a:b] for a, b in zip([0] + cuts[:-1], cuts))
    wi = w_in[:, cuts[-1]:cuts[-1] + H]
    wf = w_in[:, cuts[-1] + H:cuts[-1] + 2 * H]
    ws = w_in[:, cuts[-1] + 2 * H:]

    def pad_heads(w, scale):
        w = (w * scale).reshape(D, H, dk)
        return jnp.pad(w, ((0, 0), (0, 0), (0, LANES - dk))).reshape(D, H * LANES)

    w_if = jnp.concatenate([wi, wf], axis=1)
    w_all = jnp.concatenate(
        [pad_heads(wq, 1.0), pad_heads(wk, dk ** -0.5), wv, wo, ws,
         jnp.pad(w_if, ((0, 0), (0, LANES - 2 * H)))], axis=1).astype(BF16)
    nz = w_all.shape[1]
    w_if_t = w_if.T.astype(BF16)
    b_if = jnp.concatenate([b_igate, b_fgate]).astype(F32)
    b_col = jnp.pad(b_if, (0, LANES - 2 * H)).reshape(1, LANES)
    b_row = b_if.reshape(2 * H, 1)

    const = lambda b, j: (0, 0)
    kern = functools.partial(_mixer_ab_kernel, alpha=alpha, sconv_w=sw)
    return pl.pallas_call(
        kern,
        grid=(Bn, S // SEQ_TILE),
        in_specs=[
            pl.BlockSpec((None, SEQ_TILE, D), lambda b, j: (b, j, 0)),
            pl.BlockSpec((D, nz), const),
            pl.BlockSpec((2 * H, D), const),
            pl.BlockSpec((1, LANES), const),
            pl.BlockSpec((2 * H, 1), const),
            pl.BlockSpec((conv_w.shape[0], sw), const),
            pl.BlockSpec((1, hw), const),
            pl.BlockSpec((D, D), const),
            pl.BlockSpec((1, D), const),
            pl.BlockSpec((1, D), const),
        ],
        out_specs=pl.BlockSpec((None, SEQ_TILE, D), lambda b, j: (b, j, 0)),
        out_shape=jax.ShapeDtypeStruct((Bn, S, D), F32),
        scratch_shapes=[
            pltpu.VMEM((SEQ_TILE, nz), F32),
            pltpu.VMEM((SEQ_TILE, D), BF16),
            pltpu.VMEM((SEQ_TILE + 2 * SUBLANES, sw), F32),
            pltpu.VMEM((H, LANES, dv), F32),
            pltpu.VMEM((H, 1, LANES), F32),
            pltpu.VMEM((H, 1, LANES), F32),
        ],
        compiler_params=pltpu.CompilerParams(
            dimension_semantics=("parallel", "arbitrary"), vmem_limit_bytes=VMEM_LIMIT),
        name="mixer_ab",
    )(x, w_all, w_if_t, b_col, b_row, conv_w.astype(F32), head_gain.reshape(1, hw).astype(F32),
      w_out.astype(BF16), ln_g.reshape(1, D), ln_b.reshape(1, D))


def _conformer_kernel(x_ref, w1_ref, wdw_ref, bdw_ref, cg_ref, cb_ref, w2_ref, lng_ref, lnb_ref,
                      out_ref, ubuf_ref, uc_ref, sh_ref, *, alpha, taps, hist):
    TS = SEQ_TILE
    C = uc_ref.shape[1]

    @pl.when(pl.program_id(1) == 0)
    def _():
        ubuf_ref[0:hist, :] = jnp.zeros((hist, C), F32)

    x = x_ref[...]
    z = _dot(x.astype(BF16), w1_ref[...])
    ubuf_ref[hist:hist + TS, :] = z[:, :C] * jax.nn.sigmoid(z[:, C:])

    base = hist - (taps - 1)
    sh_rows = sh_ref.shape[1]
    for cb in range(C // LANES):
        cs = slice(cb * LANES, (cb + 1) * LANES)
        for r in range(1, SUBLANES):
            sh_ref[r - 1] = ubuf_ref[r:r + sh_rows, cs]
        for rb in range(TS // CONV_ROWS):
            acc = jnp.zeros((CONV_ROWS, LANES), F32) + bdw_ref[:, cs]
            for j in range(taps):
                r = (base + j) % SUBLANES
                start = rb * CONV_ROWS + base + j - r
                if r == 0:
                    win = ubuf_ref[start:start + CONV_ROWS, cs]
                else:
                    win = sh_ref[r - 1, start:start + CONV_ROWS, :]
                acc = acc + wdw_ref[j:j + 1, cs] * win
            uc_ref[rb * CONV_ROWS:(rb + 1) * CONV_ROWS, cs] = acc
    ubuf_ref[0:hist, :] = ubuf_ref[TS:TS + hist, :]

    un = _layer_norm(uc_ref[...], cg_ref[...], cb_ref[...])
    act = un * jax.nn.sigmoid(un)
    mix = _dot(act.astype(BF16), w2_ref[...])
    out_ref[...] = _layer_norm(alpha * x + mix, lng_ref[...], lnb_ref[...])


def _conformer(x, w_pw1, w_dw, b_dw, cg, cb, w_pw2, ln_g, ln_b, alpha):
    Bn, S, D = x.shape
    taps, C = w_dw.shape
    hist = -(-(taps - 1) // SUBLANES) * SUBLANES
    assert S % SEQ_TILE == 0 and C % LANES == 0 and hist <= SEQ_TILE
    const = lambda b, j: (0, 0)
    kern = functools.partial(_conformer_kernel, alpha=alpha, taps=taps, hist=hist)
    return pl.pallas_call(
        kern,
        grid=(Bn, S // SEQ_TILE),
        in_specs=[
            pl.BlockSpec((None, SEQ_TILE, D), lambda b, j: (b, j, 0)),
            pl.BlockSpec((D, 2 * C), const),
            pl.BlockSpec((taps, C), const),
            pl.BlockSpec((1, C), const),
            pl.BlockSpec((1, C), const),
            pl.BlockSpec((1, C), const),
            pl.BlockSpec((C, D), const),
            pl.BlockSpec((1, D), const),
            pl.BlockSpec((1, D), const),
        ],
        out_specs=pl.BlockSpec((None, SEQ_TILE, D), lambda b, j: (b, j, 0)),
        out_shape=jax.ShapeDtypeStruct((Bn, S, D), F32),
        scratch_shapes=[
            pltpu.VMEM((SEQ_TILE + hist, C), F32),
            pltpu.VMEM((SEQ_TILE, C), F32),
            pltpu.VMEM((SUBLANES - 1, SEQ_TILE + hist - SUBLANES, LANES), F32),
        ],
        compiler_params=pltpu.CompilerParams(
            dimension_semantics=("parallel", "arbitrary"), vmem_limit_bytes=VMEM_LIMIT),
        name="conformer",
    )(x, w_pw1.astype(BF16), w_dw.astype(F32), b_dw.reshape(1, C), cg.reshape(1, C),
      cb.reshape(1, C), w_pw2.astype(BF16), ln_g.reshape(1, D), ln_b.reshape(1, D))


def _router_kernel(x_ref, rwt_ref, rb_ref, idx_ref, gate_ref, rank_ref, cnt_ref, base_ref):
    E = rwt_ref.shape[0]
    TT = x_ref.shape[0]

    @pl.when(pl.program_id(0) == 0)
    def _():
        base_ref[...] = jnp.zeros_like(base_ref)

    logits = lax.dot_general(rwt_ref[...], x_ref[...], (((1,), (1,)), ((), ())),
                             preferred_element_type=F32,
                             precision=lax.Precision.HIGHEST) + rb_ref[...]
    e_iota = lax.broadcasted_iota(jnp.int32, (E, TT), 0)
    work = logits
    vals, idxs, hots = [], [], []
    for _ in range(TOP_K):
        mx = jnp.max(work, axis=0, keepdims=True)
        ix = jnp.min(jnp.where(work == mx, e_iota, E), axis=0, keepdims=True)
        hot = e_iota == ix
        work = jnp.where(hot, -jnp.inf, work)
        vals.append(mx)
        idxs.append(ix)
        hots.append(hot)
    ex = [jnp.exp(v - vals[0]) for v in vals]
    tot = functools.reduce(lambda a, b: a + b, ex)

    member = functools.reduce(lambda a, b: a | b, hots).astype(F32)
    s_i = lax.broadcasted_iota(jnp.int32, (TT, TT), 0)
    t_i = lax.broadcasted_iota(jnp.int32, (TT, TT), 1)
    before = (s_i < t_i).astype(BF16)
    pos = _dot(member.astype(BF16), before) + base_ref[:, 0:1]
    for kk in range(TOP_K):
        idx_ref[kk:kk + 1, :] = idxs[kk]
        gate_ref[kk:kk + 1, :] = ex[kk] / tot
        rank_ref[kk:kk + 1, :] = jnp.sum(jnp.where(hots[kk], pos, 0.0), axis=0,
                                         keepdims=True).astype(jnp.int32)
    base_ref[...] = base_ref[...] + jnp.sum(member, axis=1, keepdims=True)
    cnt_ref[...] = base_ref[...].astype(jnp.int32)


def _router(xt, router_w, router_b):
    T, D = xt.shape
    E = router_w.shape[1]
    assert T % ROUTER_TILE == 0
    tok = lambda i: (0, i)
    return pl.pallas_call(
        _router_kernel,
        grid=(T // ROUTER_TILE,),
        in_specs=[
            pl.BlockSpec((ROUTER_TILE, D), lambda i: (i, 0)),
            pl.BlockSpec((E, D), lambda i: (0, 0)),
            pl.BlockSpec((E, 1), lambda i: (0, 0)),
        ],
        out_specs=[
            pl.BlockSpec((TOP_K, ROUTER_TILE), tok),
            pl.BlockSpec((TOP_K, ROUTER_TILE), tok),
            pl.BlockSpec((TOP_K, ROUTER_TILE), tok),
            pl.BlockSpec((E, LANES), lambda i: (0, 0)),
        ],
        out_shape=[
            jax.ShapeDtypeStruct((TOP_K, T), jnp.int32),
            jax.ShapeDtypeStruct((TOP_K, T), F32),
            jax.ShapeDtypeStruct((TOP_K, T), jnp.int32),
            jax.ShapeDtypeStruct((E, LANES), jnp.int32),
        ],
        scratch_shapes=[pltpu.VMEM((E, LANES), F32)],
        compiler_params=pltpu.CompilerParams(
            dimension_semantics=("arbitrary",), vmem_limit_bytes=VMEM_LIMIT),
        name="router",
    )(xt, router_w.T.astype(F32), router_b.reshape(E, 1).astype(F32))


def _row_tile(i):
    return pl.ds(pl.multiple_of(i * SUBLANES, SUBLANES), SUBLANES)


def _to_row_tiles(dst_ref, val):
    n = val.shape[0]
    for s in range(SUBLANES):
        dst_ref[pl.ds(s, n, stride=SUBLANES), :] = val[:, s * LANES:(s + 1) * LANES]


def _from_row_tiles(src_ref, n):
    return jnp.concatenate(
        [src_ref[pl.ds(s, n, stride=SUBLANES), :] for s in range(SUBLANES)], axis=-1)


def _dispatch_kernel(pad_start_ref, pad_cnt_ref, nbu_ref, dest_ref, x_ref, xs_ref, src_ref, zblk_ref,
                     sems):
    TT = x_ref.shape[0]
    E = pad_start_ref.shape[0]
    RT = zblk_ref.shape[0]
    nb = xs_ref.shape[0] // RT

    _to_row_tiles(src_ref, x_ref[...])

    def for_each_row(fn):
        def body(t, carry):
            for kk in range(TOP_K):
                d = dest_ref[0, 0, kk * TT + t]
                fn(pltpu.make_async_copy(src_ref.at[_row_tile(t), :], xs_ref.at[_row_tile(d), :],
                                         sems.at[kk % DMA_QUEUES]), kk % DMA_QUEUES)
            return carry
        lax.fori_loop(0, TT, body, 0)

    for_each_row(lambda cp, q: cp.start(priority=q))
    for_each_row(lambda cp, q: cp.wait())

    @pl.when(pl.program_id(0) == pl.num_programs(0) - 1)
    def _():
        zblk_ref[...] = jnp.zeros_like(zblk_ref)
        sem = sems.at[0]

        def pad_copy(e, r):
            return pltpu.make_async_copy(zblk_ref.at[pl.ds(0, SUBLANES), :],
                                         xs_ref.at[_row_tile(pad_start_ref[e] + r), :], sem)

        def tail_copy(blk):
            return pltpu.make_async_copy(
                zblk_ref, xs_ref.at[pl.ds(pl.multiple_of(blk * RT, RT), RT), :], sem)

        def for_each_pad(fn):
            def per_expert(e, carry):
                def per_row(r, c):
                    fn(pad_copy(e, r))
                    return c
                lax.fori_loop(0, pad_cnt_ref[e], per_row, 0)
                return carry
            lax.fori_loop(0, E, per_expert, 0)

            def per_block(blk, carry):
                fn(tail_copy(blk))
                return carry
            lax.fori_loop(nbu_ref[0], nb, per_block, 0)

        for_each_pad(lambda cp: cp.start())
        for_each_pad(lambda cp: cp.wait())


def _dispatch(xt, dest_tiles, pad_start, pad_cnt, nb_used, n_rows):
    T, D = xt.shape
    nt = T // DISPATCH_TILE
    grid_spec = pltpu.PrefetchScalarGridSpec(
        num_scalar_prefetch=3,
        grid=(nt,),
        in_specs=[
            pl.BlockSpec((1, 1, TOP_K * DISPATCH_TILE), lambda i, ps, pc, nbu: (i, 0, 0),
                         memory_space=pltpu.SMEM),
            pl.BlockSpec((DISPATCH_TILE, D), lambda i, ps, pc, nbu: (i, 0)),
        ],
        out_specs=pl.BlockSpec(memory_space=pl.ANY),
        scratch_shapes=[pltpu.VMEM((DISPATCH_TILE * SUBLANES, LANES), F32),
                        pltpu.VMEM((EXPERT_ROWS * SUBLANES, LANES), F32),
                        pltpu.SemaphoreType.DMA((DMA_QUEUES,))],
    )
    assert D == SUBLANES * LANES
    return pl.pallas_call(
        _dispatch_kernel,
        grid_spec=grid_spec,
        out_shape=jax.ShapeDtypeStruct((n_rows * SUBLANES, LANES), F32),
        compiler_params=pltpu.CompilerParams(
            dimension_semantics=("arbitrary",), vmem_limit_bytes=VMEM_LIMIT),
        name="dispatch",
    )(pad_start, pad_cnt, nb_used, dest_tiles, xt)


def _cast_rows(src_ref, dst_ref):
    rows = src_ref.shape[0]

    def body(i, carry):
        r = pl.multiple_of(i * CAST_ROWS, CAST_ROWS)
        dst_ref[pl.ds(r, CAST_ROWS), :] = src_ref[pl.ds(r, CAST_ROWS), :].astype(dst_ref.dtype)
        return carry

    lax.fori_loop(0, rows // CAST_ROWS, body, 0)


def _expert_kernel(ge_ref, nbu_ref, xs_ref, wu_ref, bu_ref, wd_ref, bd_ref, ys_ref, wub_ref, wdb_ref):
    F = wd_ref.shape[0]
    b = pl.program_id(0)

    @pl.when(b >= nbu_ref[0])
    def _():
        ys_ref[...] = jnp.zeros_like(ys_ref)

    @pl.when(b < nbu_ref[0])
    def _():
        @pl.when((b == 0) | (ge_ref[b] != ge_ref[jnp.maximum(b - 1, 0)]))
        def _():
            _cast_rows(wu_ref, wub_ref)
            _cast_rows(wd_ref, wdb_ref)

        xs = _from_row_tiles(xs_ref, EXPERT_ROWS)
        hb = _dot(xs.astype(BF16), wub_ref[...]) + bu_ref[...]
        glu = jnp.minimum(hb[:, :F], SWIGLU_LIMIT)
        lin = jnp.clip(hb[:, F:], -SWIGLU_LIMIT, SWIGLU_LIMIT)
        act = glu * jax.nn.sigmoid(SWIGLU_ALPHA * glu) * (lin + 1.0)
        _to_row_tiles(ys_ref, _dot(act.astype(BF16), wdb_ref[...]) + bd_ref[...])


def _experts(xs, group_expert, nb_used, layer, w_up, b_up, w_down, b_down):
    depth, E, D, F2 = w_up.shape
    F = w_down.shape[2]
    RT = EXPERT_ROWS * SUBLANES
    nb = xs.shape[0] // RT
    assert D % CAST_ROWS == 0 and F % CAST_ROWS == 0
    row_block = lambda b, ge, nbu: (jnp.minimum(b, nbu[0] - 1), 0)
    expert_block = lambda b, ge, nbu: (layer, ge[b], 0, 0)
    grid_spec = pltpu.PrefetchScalarGridSpec(
        num_scalar_prefetch=2,
        grid=(nb,),
        in_specs=[
            pl.BlockSpec((RT, LANES), row_block),
            pl.BlockSpec((None, None, D, F2), expert_block),
            pl.BlockSpec((None, None, 1, F2), expert_block),
            pl.BlockSpec((None, None, F, D), expert_block),
            pl.BlockSpec((None, None, 1, D), expert_block),
        ],
        out_specs=pl.BlockSpec((RT, LANES), lambda b, ge, nbu: (b, 0)),
        scratch_shapes=[pltpu.VMEM((D, F2), BF16), pltpu.VMEM((F, D), BF16)],
    )
    return pl.pallas_call(
        _expert_kernel,
        grid_spec=grid_spec,
        out_shape=jax.ShapeDtypeStruct(xs.shape, F32),
        compiler_params=pltpu.CompilerParams(
            dimension_semantics=("arbitrary",), vmem_limit_bytes=VMEM_LIMIT),
        name="experts",
    )(group_expert, nb_used, xs, w_up, b_up.reshape(depth, E, 1, F2), w_down,
      b_down.reshape(depth, E, 1, D))


def _combine_kernel(dest_ref, dnext_ref, gate_ref, x_ref, ys_ref, lng_ref, lnb_ref, out_ref,
                    buf_ref, sems, *, alpha):
    TT = x_ref.shape[0]
    i = pl.program_id(0)
    slot = lax.rem(i, 2)

    def for_each_row(idx_ref, s, fn):
        def body(t, carry):
            for kk in range(TOP_K):
                d = idx_ref[0, 0, kk * TT + t]
                q = kk % DMA_QUEUES
                fn(pltpu.make_async_copy(ys_ref.at[_row_tile(d), :],
                                         buf_ref.at[s, kk, _row_tile(t), :], sems.at[s, q]), q)
            return carry
        lax.fori_loop(0, TT, body, 0)

    @pl.when(i == 0)
    def _():
        for_each_row(dest_ref, 0, lambda cp, q: cp.start(priority=q))

    @pl.when(i + 1 < pl.num_programs(0))
    def _():
        for_each_row(dnext_ref, 1 - slot, lambda cp, q: cp.start(priority=q))

    for_each_row(dest_ref, slot, lambda cp, q: cp.wait())
    f = gate_ref[:, 0:1] * _from_row_tiles(buf_ref.at[slot, 0], TT)
    for kk in range(1, TOP_K):
        f = f + gate_ref[:, kk:kk + 1] * _from_row_tiles(buf_ref.at[slot, kk], TT)
    out_ref[...] = _layer_norm(alpha * x_ref[...] + f, lng_ref[...], lnb_ref[...])


def _combine(xt, ys, dest_tiles, gates_col, ln_g, ln_b, alpha):
    T, D = xt.shape
    nt = T // COMBINE_TILE
    kern = functools.partial(_combine_kernel, alpha=alpha)
    slots = (1, 1, TOP_K * COMBINE_TILE)
    return pl.pallas_call(
        kern,
        grid=(nt,),
        in_specs=[
            pl.BlockSpec(slots, lambda i: (i, 0, 0), memory_space=pltpu.SMEM),
            pl.BlockSpec(slots, lambda i: (jnp.minimum(i + 1, nt - 1), 0, 0),
                         memory_space=pltpu.SMEM),
            pl.BlockSpec((COMBINE_TILE, TOP_K), lambda i: (i, 0)),
            pl.BlockSpec((COMBINE_TILE, D), lambda i: (i, 0)),
            pl.BlockSpec(memory_space=pl.ANY),
            pl.BlockSpec((1, D), lambda i: (0, 0)),
            pl.BlockSpec((1, D), lambda i: (0, 0)),
        ],
        out_specs=pl.BlockSpec((COMBINE_TILE, D), lambda i: (i, 0)),
        out_shape=jax.ShapeDtypeStruct((T, D), F32),
        scratch_shapes=[pltpu.VMEM((2, TOP_K, COMBINE_TILE * SUBLANES, LANES), F32),
                        pltpu.SemaphoreType.DMA((2, DMA_QUEUES))],
        compiler_params=pltpu.CompilerParams(
            dimension_semantics=("arbitrary",), vmem_limit_bytes=VMEM_LIMIT),
        name="combine",
    )(dest_tiles, dest_tiles, gates_col, xt, ys, ln_g.reshape(1, D), ln_b.reshape(1, D))


def _tile_slots(dest, tile):
    K, T = dest.shape
    return dest.reshape(K, T // tile, tile).transpose(1, 0, 2).reshape(T // tile, 1, K * tile)


def _moe(xt, layer, router_w, router_b, w_up, b_up, w_down, b_down, ln_g, ln_b, alpha):
    T, D = xt.shape
    E = router_w.shape[1]
    R = EXPERT_ROWS
    idx, gates, rank, cnt = _router(xt, router_w, router_b)
    counts = cnt[:, 0]
    padded = ((counts + R - 1) // R) * R
    pends = jnp.cumsum(padded)
    pstarts = pends - padded
    e_ids = jnp.arange(E, dtype=jnp.int32)[:, None, None]
    dest = rank + jnp.sum(jnp.where(idx[None] == e_ids, pstarts[:, None, None], 0), axis=0)
    n_rows = -(-(T * TOP_K + E * (R - 1)) // R) * R
    nb = n_rows // R
    nb_used = (pends[E - 1:] // R).astype(jnp.int32)
    blk = jnp.minimum(jnp.arange(nb, dtype=jnp.int32), nb_used - 1) * R
    group_expert = jnp.sum((pends[None, :] <= blk[:, None]).astype(jnp.int32), axis=1)
    xs = _dispatch(xt, _tile_slots(dest, DISPATCH_TILE), (pstarts + counts).astype(jnp.int32),
                   (padded - counts).astype(jnp.int32), nb_used, n_rows)
    ys = _experts(xs, group_expert, nb_used, layer, w_up, b_up, w_down, b_down)
    return _combine(xt, ys, _tile_slots(dest, COMBINE_TILE), gates.T, ln_g, ln_b, alpha)


def kernel(x, ab_w_in, ab_b_igate, ab_b_fgate, ab_conv_w, ab_head_gain, ab_w_out, cf_w_pw1, cf_w_dw,
           cf_b_dw, cf_ln_g, cf_ln_b, cf_w_pw2, router_w, router_b, exp_w_up, exp_b_up, exp_w_down,
           exp_b_down, post_ln_g, post_ln_b):
    Bn, S, D = x.shape
    depth = post_ln_g.shape[0]
    alpha = float((2 * depth) ** 0.25)
    for layer in range(depth):
        j = layer // 2
        if layer % 2 == 0:
            x = _mixer_ab(x, ab_w_in[j], ab_b_igate[j], ab_b_fgate[j], ab_conv_w[j], ab_head_gain[j],
                          ab_w_out[j], post_ln_g[layer, 0], post_ln_b[layer, 0], alpha)
        else:
            x = _conformer(x, cf_w_pw1[j], cf_w_dw[j], cf_b_dw[j], cf_ln_g[j], cf_ln_b[j],
                           cf_w_pw2[j], post_ln_g[layer, 0], post_ln_b[layer, 0], alpha)
        xt = _moe(x.reshape(Bn * S, D), layer, router_w[layer], router_b[layer], exp_w_up,
                  exp_b_up, exp_w_down, exp_b_down, post_ln_g[layer, 1], post_ln_b[layer, 1], alpha)
        x = xt.reshape(Bn, S, D)
    return x
```

```python
import functools

import jax
import jax.numpy as jnp
from jax import lax
from jax.experimental import pallas as pl
from jax.experimental.pallas import tpu as pltpu

F32 = jnp.float32
BF16 = jnp.bfloat16

LANES = 128
SUBLANES = 8

MLSTM_HEADS = 4
MLSTM_QK_DIM = 64
MLSTM_V_DIM = 128
TOP_K = 4
SWIGLU_LIMIT = 7.0
SWIGLU_ALPHA = 1.702
LN_EPS = 1e-5
HEAD_NORM_EPS = 1e-6

SEQ_TILE = 512
MLSTM_CHUNK = 128
ROUTER_TILE = 512
DISPATCH_TILE = 256
COMBINE_TILE = 256
EXPERT_ROWS = 512
CONV_ROWS = 64
CAST_ROWS = 128
DMA_QUEUES = 2
VMEM_LIMIT = 56 * 1024 * 1024


def _layer_norm(y, g, b):
    mu = jnp.mean(y, axis=-1, keepdims=True)
    yc = y - mu
    var = jnp.mean(yc * yc, axis=-1, keepdims=True)
    return yc * lax.rsqrt(var + LN_EPS) * g + b


def _log_sigmoid(x):
    return jnp.minimum(x, 0.0) - jnp.log1p(jnp.exp(-jnp.abs(x)))


def _dot(a, b):
    return jnp.dot(a, b, preferred_element_type=F32)


def _dot_nt(a, b):
    return lax.dot_general(a, b, (((1,), (1,)), ((), ())), preferred_element_type=F32)


def _dot_tn(a, b):
    return lax.dot_general(a, b, (((0,), (0,)), ((), ())), preferred_element_type=F32)


def _dot_f32(a, b):
    return jnp.dot(a, b, preferred_element_type=F32, precision=lax.Precision.HIGHEST)


def _mixer_ab_kernel(x_ref, wall_ref, wift_ref, bcol_ref, brow_ref, convw_ref, gain_ref, wout_ref,
                     lng_ref, lnb_ref, out_ref,
                     z_ref, hy_ref, ubuf_ref, c_ref, n_ref, m_ref, *, alpha, sconv_w):
    H, L, TS = MLSTM_HEADS, MLSTM_CHUNK, SEQ_TILE
    HW = H * LANES
    q0, k0, v0, o0 = 0, HW, 2 * HW, 3 * HW
    sb0 = 4 * HW
    sc0 = sb0 + sconv_w
    sx0 = sc0 + sconv_w
    if0 = sx0 + sconv_w

    @pl.when(pl.program_id(1) == 0)
    def _():
        c_ref[...] = jnp.zeros_like(c_ref)
        n_ref[...] = jnp.zeros_like(n_ref)
        m_ref[...] = jnp.zeros_like(m_ref)
        ubuf_ref[0:SUBLANES, :] = jnp.zeros((SUBLANES, sconv_w), F32)

    x = x_ref[...]
    xb = x.astype(BF16)
    z_ref[...] = _dot(xb, wall_ref[...])
    pre_row = _dot_nt(wift_ref[...], xb) + brow_ref[...]
    logf_row = _log_sigmoid(pre_row)
    pre_col = z_ref[:, if0:if0 + LANES] + bcol_ref[...]
    logf_col = _log_sigmoid(pre_col)

    r_i = lax.broadcasted_iota(jnp.int32, (L, L), 0)
    c_i = lax.broadcasted_iota(jnp.int32, (L, L), 1)
    causal = r_i >= c_i
    tril = causal.astype(F32)
    triu = (r_i <= c_i).astype(F32)

    for c in range(TS // L):
        r0 = c * L
        b_col = _dot_f32(tril, logf_col[r0:r0 + L, :])
        b_row = _dot_f32(logf_row[:, r0:r0 + L], triu)
        for h in range(H):
            q = z_ref[r0:r0 + L, q0 + h * LANES:q0 + (h + 1) * LANES]
            k = z_ref[r0:r0 + L, k0 + h * LANES:k0 + (h + 1) * LANES]
            v = z_ref[r0:r0 + L, v0 + h * LANES:v0 + (h + 1) * LANES]
            o = z_ref[r0:r0 + L, o0 + h * LANES:o0 + (h + 1) * LANES]
            qb, kb, vb = q.astype(BF16), k.astype(BF16), v.astype(BF16)
            bc = b_col[:, H + h:H + h + 1]
            ic = pre_col[r0:r0 + L, h:h + 1]
            br = b_row[H + h:H + h + 1, :]
            ir = pre_row[h:h + 1, r0:r0 + L]
            c_prev = c_ref[h]
            n_prev = n_ref[h]
            m_prev = m_ref[h][:, 0:1]

            dmat = jnp.where(causal, bc - br + ir, -jnp.inf)
            e_inter = bc + m_prev
            m_t = jnp.maximum(e_inter, jnp.max(dmat, axis=-1, keepdims=True))
            w_intra = jnp.exp(dmat - m_t)
            w_inter = jnp.exp(e_inter - m_t)
            s = _dot_nt(qb, kb) * w_intra
            num = w_inter * _dot(qb, c_prev.astype(BF16)) + _dot(s.astype(BF16), vb)
            den = (w_inter * jnp.sum(q * n_prev, axis=-1, keepdims=True)
                   + jnp.sum(s, axis=-1, keepdims=True))
            hh = num / jnp.maximum(jnp.abs(den), jnp.exp(-m_t))

            g = bc[L - 1:L, :]
            a_col = g - bc + ic
            m_new = jnp.maximum(g + m_prev, jnp.max(a_col, axis=0, keepdims=True))
            decay = jnp.exp(g + m_prev - m_new)
            kw = k * jnp.exp(a_col - m_new)
            c_ref[h] = decay * c_prev + _dot_tn(kw.astype(BF16), vb)
            n_ref[h] = decay * n_prev + jnp.sum(kw, axis=0, keepdims=True)
            m_ref[h] = jnp.broadcast_to(m_new, (1, LANES))

            mu = jnp.mean(hh, axis=-1, keepdims=True)
            hc = hh - mu
            var = jnp.mean(hc * hc, axis=-1, keepdims=True)
            hn = hc * lax.rsqrt(var + HEAD_NORM_EPS) * gain_ref[:, h * LANES:(h + 1) * LANES]
            hy_ref[r0:r0 + L, h * LANES:(h + 1) * LANES] = (jax.nn.sigmoid(o) * hn).astype(BF16)

    u = z_ref[:, sc0:sc0 + sconv_w] * z_ref[:, sx0:sx0 + sconv_w]
    ubuf_ref[SUBLANES:SUBLANES + TS, :] = u
    conv = (convw_ref[2:3, :] * u
            + convw_ref[1:2, :] * ubuf_ref[SUBLANES - 1:SUBLANES - 1 + TS, :]
            + convw_ref[0:1, :] * ubuf_ref[SUBLANES - 2:SUBLANES - 2 + TS, :])
    hy_ref[:, HW:HW + sconv_w] = (z_ref[:, sb0:sb0 + sconv_w] * conv).astype(BF16)
    ubuf_ref[0:SUBLANES, :] = ubuf_ref[TS:TS + SUBLANES, :]

    mix = _dot(hy_ref[...], wout_ref[...])
    out_ref[...] = _layer_norm(alpha * x + mix, lng_ref[...], lnb_ref[...])


def _mixer_ab(x, w_in, b_igate, b_fgate, conv_w, head_gain, w_out, ln_g, ln_b, alpha):
    Bn, S, D = x.shape
    H, dk, dv = MLSTM_HEADS, MLSTM_QK_DIM, MLSTM_V_DIM
    assert dv == LANES and dk <= LANES and S % SEQ_TILE == 0 and SEQ_TILE % MLSTM_CHUNK == 0
    hw = H * dv
    sw = D - hw
    cuts = [H * dk, 2 * H * dk, 2 * H * dk + hw, 2 * H * dk + 2 * hw]
    wq, wk, wv, wo = (w_in[:, a:b] for a, b in zip([0] + cuts[:-1], cuts))
    wi = w_in[:, cuts[-1]:cuts[-1] + H]
    wf = w_in[:, cuts[-1] + H:cuts[-1] + 2 * H]
    ws = w_in[:, cuts[-1] + 2 * H:]

    def pad_heads(w, scale):
        w = (w * scale).reshape(D, H, dk)
        return jnp.pad(w, ((0, 0), (0, 0), (0, LANES - dk))).reshape(D, H * LANES)

    w_if = jnp.concatenate([wi, wf], axis=1)
    w_all = jnp.concatenate(
        [pad_heads(wq, 1.0), pad_heads(wk, dk ** -0.5), wv, wo, ws,
         jnp.pad(w_if, ((0, 0), (0, LANES - 2 * H)))], axis=1).astype(BF16)
    nz = w_all.shape[1]
    w_if_t = w_if.T.astype(BF16)
    b_if = jnp.concatenate([b_igate, b_fgate]).astype(F32)
    b_col = jnp.pad(b_if, (0, LANES - 2 * H)).reshape(1, LANES)
    b_row = b_if.reshape(2 * H, 1)

    const = lambda b, j: (0, 0)
    kern = functools.partial(_mixer_ab_kernel, alpha=alpha, sconv_w=sw)
    return pl.pallas_call(
        kern,
        grid=(Bn, S // SEQ_TILE),
        in_specs=[
            pl.BlockSpec((None, SEQ_TILE, D), lambda b, j: (b, j, 0)),
            pl.BlockSpec((D, nz), const),
            pl.BlockSpec((2 * H, D), const),
            pl.BlockSpec((1, LANES), const),
            pl.BlockSpec((2 * H, 1), const),
            pl.BlockSpec((conv_w.shape[0], sw), const),
            pl.BlockSpec((1, hw), const),
            pl.BlockSpec((D, D), const),
            pl.BlockSpec((1, D), const),
            pl.BlockSpec((1, D), const),
        ],
        out_specs=pl.BlockSpec((None, SEQ_TILE, D), lambda b, j: (b, j, 0)),
        out_shape=jax.ShapeDtypeStruct((Bn, S, D), F32),
        scratch_shapes=[
            pltpu.VMEM((SEQ_TILE, nz), F32),
            pltpu.VMEM((SEQ_TILE, D), BF16),
            pltpu.VMEM((SEQ_TILE + 2 * SUBLANES, sw), F32),
            pltpu.VMEM((H, LANES, dv), F32),
            pltpu.VMEM((H, 1, LANES), F32),
            pltpu.VMEM((H, 1, LANES), F32),
        ],
        compiler_params=pltpu.CompilerParams(
            dimension_semantics=("parallel", "arbitrary"), vmem_limit_bytes=VMEM_LIMIT),
        name="mixer_ab",
    )(x, w_all, w_if_t, b_col, b_row, conv_w.astype(F32), head_gain.reshape(1, hw).astype(F32),
      w_out.astype(BF16), ln_g.reshape(1, D), ln_b.reshape(1, D))


def _conformer_kernel(dest_ref, dnext_ref, gate_ref, ys_ref, x_ref, mlng_ref, mlnb_ref,
                      w1_ref, wdw_ref, bdw_ref, cg_ref, cb_ref, w2_ref, lng_ref, lnb_ref,
                      out_ref, ubuf_ref, uc_ref, sh_ref, buf_ref, sems, *, alpha, taps, hist):
    TS = SEQ_TILE
    C = uc_ref.shape[1]

    @pl.when(pl.program_id(1) == 0)
    def _():
        ubuf_ref[0:hist, :] = jnp.zeros((hist, C), F32)

    step = pl.program_id(0) * pl.num_programs(1) + pl.program_id(1)
    f = _moe_output(step, pl.num_programs(0) * pl.num_programs(1), dest_ref, dnext_ref, gate_ref,
                    ys_ref, buf_ref, sems)
    x = _layer_norm(alpha * x_ref[...] + f, mlng_ref[...], mlnb_ref[...])
    z = _dot(x.astype(BF16), w1_ref[...])
    ubuf_ref[hist:hist + TS, :] = z[:, :C] * jax.nn.sigmoid(z[:, C:])

    base = hist - (taps - 1)
    sh_rows = sh_ref.shape[1]
    for cb in range(C // LANES):
        cs = slice(cb * LANES, (cb + 1) * LANES)
        for r in range(1, SUBLANES):
            sh_ref[r - 1] = ubuf_ref[r:r + sh_rows, cs]
        for rb in range(TS // CONV_ROWS):
            acc = jnp.zeros((CONV_ROWS, LANES), F32) + bdw_ref[:, cs]
            for j in range(taps):
                r = (base + j) % SUBLANES
                start = rb * CONV_ROWS + base + j - r
                if r == 0:
                    win = ubuf_ref[start:start + CONV_ROWS, cs]
                else:
                    win = sh_ref[r - 1, start:start + CONV_ROWS, :]
                acc = acc + wdw_ref[j:j + 1, cs] * win
            uc_ref[rb * CONV_ROWS:(rb + 1) * CONV_ROWS, cs] = acc
    ubuf_ref[0:hist, :] = ubuf_ref[TS:TS + hist, :]

    un = _layer_norm(uc_ref[...], cg_ref[...], cb_ref[...])
    act = un * jax.nn.sigmoid(un)
    mix = _dot(act.astype(BF16), w2_ref[...])
    out_ref[...] = _layer_norm(alpha * x + mix, lng_ref[...], lnb_ref[...])


def _conformer(x, moe_out, moe_ln_g, moe_ln_b, w_pw1, w_dw, b_dw, cg, cb, w_pw2, ln_g, ln_b, alpha):
    ys, dest, gates_col = moe_out
    Bn, S, D = x.shape
    taps, C = w_dw.shape
    hist = -(-(taps - 1) // SUBLANES) * SUBLANES
    assert S % SEQ_TILE == 0 and C % LANES == 0 and hist <= SEQ_TILE
    nj = S // SEQ_TILE
    const = lambda b, j: (0, 0)
    dest_tiles = _tile_slots(dest, SEQ_TILE)
    gather_specs, gather_scratch = _moe_gather_specs(SEQ_TILE, Bn * nj, lambda b, j: b * nj + j)
    kern = functools.partial(_conformer_kernel, alpha=alpha, taps=taps, hist=hist)
    return pl.pallas_call(
        kern,
        grid=(Bn, nj),
        in_specs=gather_specs + [
            pl.BlockSpec((None, SEQ_TILE, D), lambda b, j: (b, j, 0)),
            pl.BlockSpec((1, D), const),
            pl.BlockSpec((1, D), const),
            pl.BlockSpec((D, 2 * C), const),
            pl.BlockSpec((taps, C), const),
            pl.BlockSpec((1, C), const),
            pl.BlockSpec((1, C), const),
            pl.BlockSpec((1, C), const),
            pl.BlockSpec((C, D), const),
            pl.BlockSpec((1, D), const),
            pl.BlockSpec((1, D), const),
        ],
        out_specs=pl.BlockSpec((None, SEQ_TILE, D), lambda b, j: (b, j, 0)),
        out_shape=jax.ShapeDtypeStruct((Bn, S, D), F32),
        scratch_shapes=[
            pltpu.VMEM((SEQ_TILE + hist, C), F32),
            pltpu.VMEM((SEQ_TILE, C), F32),
            pltpu.VMEM((SUBLANES - 1, SEQ_TILE + hist - SUBLANES, LANES), F32),
        ] + gather_scratch,
        compiler_params=pltpu.CompilerParams(
            dimension_semantics=("arbitrary", "arbitrary"), vmem_limit_bytes=VMEM_LIMIT),
        name="conformer",
    )(dest_tiles, dest_tiles, gates_col, ys, x, moe_ln_g.reshape(1, D), moe_ln_b.reshape(1, D),
      w_pw1.astype(BF16), w_dw.astype(F32), b_dw.reshape(1, C), cg.reshape(1, C),
      cb.reshape(1, C), w_pw2.astype(BF16), ln_g.reshape(1, D), ln_b.reshape(1, D))


def _router_kernel(x_ref, rwt_ref, rb_ref, idx_ref, gate_ref, rank_ref, cnt_ref, base_ref):
    E = rwt_ref.shape[0]
    TT = x_ref.shape[0]

    @pl.when(pl.program_id(0) == 0)
    def _():
        base_ref[...] = jnp.zeros_like(base_ref)

    logits = lax.dot_general(rwt_ref[...], x_ref[...], (((1,), (1,)), ((), ())),
                             preferred_element_type=F32,
                             precision=lax.Precision.HIGHEST) + rb_ref[...]
    e_iota = lax.broadcasted_iota(jnp.int32, (E, TT), 0)
    work = logits
    vals, idxs, hots = [], [], []
    for _ in range(TOP_K):
        mx = jnp.max(work, axis=0, keepdims=True)
        ix = jnp.min(jnp.where(work == mx, e_iota, E), axis=0, keepdims=True)
        hot = e_iota == ix
        work = jnp.where(hot, -jnp.inf, work)
        vals.append(mx)
        idxs.append(ix)
        hots.append(hot)
    ex = [jnp.exp(v - vals[0]) for v in vals]
    tot = functools.reduce(lambda a, b: a + b, ex)

    member = functools.reduce(lambda a, b: a | b, hots).astype(F32)
    s_i = lax.broadcasted_iota(jnp.int32, (TT, TT), 0)
    t_i = lax.broadcasted_iota(jnp.int32, (TT, TT), 1)
    before = (s_i < t_i).astype(BF16)
    pos = _dot(member.astype(BF16), before) + base_ref[:, 0:1]
    for kk in range(TOP_K):
        idx_ref[kk:kk + 1, :] = idxs[kk]
        gate_ref[kk:kk + 1, :] = ex[kk] / tot
        rank_ref[kk:kk + 1, :] = jnp.sum(jnp.where(hots[kk], pos, 0.0), axis=0,
                                         keepdims=True).astype(jnp.int32)
    base_ref[...] = base_ref[...] + jnp.sum(member, axis=1, keepdims=True)
    cnt_ref[...] = base_ref[...].astype(jnp.int32)


def _router(xt, router_w, router_b):
    T, D = xt.shape
    E = router_w.shape[1]
    assert T % ROUTER_TILE == 0
    tok = lambda i: (0, i)
    return pl.pallas_call(
        _router_kernel,
        grid=(T // ROUTER_TILE,),
        in_specs=[
            pl.BlockSpec((ROUTER_TILE, D), lambda i: (i, 0)),
            pl.BlockSpec((E, D), lambda i: (0, 0)),
            pl.BlockSpec((E, 1), lambda i: (0, 0)),
        ],
        out_specs=[
            pl.BlockSpec((TOP_K, ROUTER_TILE), tok),
            pl.BlockSpec((TOP_K, ROUTER_TILE), tok),
            pl.BlockSpec((TOP_K, ROUTER_TILE), tok),
            pl.BlockSpec((E, LANES), lambda i: (0, 0)),
        ],
        out_shape=[
            jax.ShapeDtypeStruct((TOP_K, T), jnp.int32),
            jax.ShapeDtypeStruct((TOP_K, T), F32),
            jax.ShapeDtypeStruct((TOP_K, T), jnp.int32),
            jax.ShapeDtypeStruct((E, LANES), jnp.int32),
        ],
        scratch_shapes=[pltpu.VMEM((E, LANES), F32)],
        compiler_params=pltpu.CompilerParams(
            dimension_semantics=("arbitrary",), vmem_limit_bytes=VMEM_LIMIT),
        name="router",
    )(xt, router_w.T.astype(F32), router_b.reshape(E, 1).astype(F32))


def _row_tile(i):
    return pl.ds(pl.multiple_of(i * SUBLANES, SUBLANES), SUBLANES)


def _to_row_tiles(dst_ref, val):
    n = val.shape[0]
    for s in range(SUBLANES):
        dst_ref[pl.ds(s, n, stride=SUBLANES), :] = val[:, s * LANES:(s + 1) * LANES]


def _from_row_tiles(src_ref, n):
    return jnp.concatenate(
        [src_ref[pl.ds(s, n, stride=SUBLANES), :] for s in range(SUBLANES)], axis=-1)


def _dispatch_kernel(pad_start_ref, pad_cnt_ref, nbu_ref, dest_ref, x_ref, xs_ref, src_ref, zblk_ref,
                     sems):
    TT = x_ref.shape[0]
    E = pad_start_ref.shape[0]
    RT = zblk_ref.shape[0]
    nb = xs_ref.shape[0] // RT

    _to_row_tiles(src_ref, x_ref[...])

    def for_each_row(fn):
        def body(t, carry):
            for kk in range(TOP_K):
                d = dest_ref[0, 0, kk * TT + t]
                fn(pltpu.make_async_copy(src_ref.at[_row_tile(t), :], xs_ref.at[_row_tile(d), :],
                                         sems.at[kk % DMA_QUEUES]), kk % DMA_QUEUES)
            return carry
        lax.fori_loop(0, TT, body, 0)

    for_each_row(lambda cp, q: cp.start(priority=q))
    for_each_row(lambda cp, q: cp.wait())

    @pl.when(pl.program_id(0) == pl.num_programs(0) - 1)
    def _():
        zblk_ref[...] = jnp.zeros_like(zblk_ref)
        sem = sems.at[0]

        def pad_copy(e, r):
            return pltpu.make_async_copy(zblk_ref.at[pl.ds(0, SUBLANES), :],
                                         xs_ref.at[_row_tile(pad_start_ref[e] + r), :], sem)

        def tail_copy(blk):
            return pltpu.make_async_copy(
                zblk_ref, xs_ref.at[pl.ds(pl.multiple_of(blk * RT, RT), RT), :], sem)

        def for_each_pad(fn):
            def per_expert(e, carry):
                def per_row(r, c):
                    fn(pad_copy(e, r))
                    return c
                lax.fori_loop(0, pad_cnt_ref[e], per_row, 0)
                return carry
            lax.fori_loop(0, E, per_expert, 0)

            def per_block(blk, carry):
                fn(tail_copy(blk))
                return carry
            lax.fori_loop(nbu_ref[0], nb, per_block, 0)

        for_each_pad(lambda cp: cp.start())
        for_each_pad(lambda cp: cp.wait())


def _dispatch(xt, dest_tiles, pad_start, pad_cnt, nb_used, n_rows):
    T, D = xt.shape
    nt = T // DISPATCH_TILE
    grid_spec = pltpu.PrefetchScalarGridSpec(
        num_scalar_prefetch=3,
        grid=(nt,),
        in_specs=[
            pl.BlockSpec((1, 1, TOP_K * DISPATCH_TILE), lambda i, ps, pc, nbu: (i, 0, 0),
                         memory_space=pltpu.SMEM),
            pl.BlockSpec((DISPATCH_TILE, D), lambda i, ps, pc, nbu: (i, 0)),
        ],
        out_specs=pl.BlockSpec(memory_space=pl.ANY),
        scratch_shapes=[pltpu.VMEM((DISPATCH_TILE * SUBLANES, LANES), F32),
                        pltpu.VMEM((EXPERT_ROWS * SUBLANES, LANES), F32),
                        pltpu.SemaphoreType.DMA((DMA_QUEUES,))],
    )
    assert D == SUBLANES * LANES
    return pl.pallas_call(
        _dispatch_kernel,
        grid_spec=grid_spec,
        out_shape=jax.ShapeDtypeStruct((n_rows * SUBLANES, LANES), F32),
        compiler_params=pltpu.CompilerParams(
            dimension_semantics=("arbitrary",), vmem_limit_bytes=VMEM_LIMIT),
        name="dispatch",
    )(pad_start, pad_cnt, nb_used, dest_tiles, xt)


def _cast_rows(src_ref, dst_ref):
    rows = src_ref.shape[0]

    def body(i, carry):
        r = pl.multiple_of(i * CAST_ROWS, CAST_ROWS)
        dst_ref[pl.ds(r, CAST_ROWS), :] = src_ref[pl.ds(r, CAST_ROWS), :].astype(dst_ref.dtype)
        return carry

    lax.fori_loop(0, rows // CAST_ROWS, body, 0)


def _expert_kernel(ge_ref, nbu_ref, xs_ref, wu_ref, bu_ref, wd_ref, bd_ref, ys_ref, wub_ref, wdb_ref):
    F = wd_ref.shape[0]
    b = pl.program_id(0)

    @pl.when(b >= nbu_ref[0])
    def _():
        ys_ref[...] = jnp.zeros_like(ys_ref)

    @pl.when(b < nbu_ref[0])
    def _():
        @pl.when((b == 0) | (ge_ref[b] != ge_ref[jnp.maximum(b - 1, 0)]))
        def _():
            _cast_rows(wu_ref, wub_ref)
            _cast_rows(wd_ref, wdb_ref)

        xs = _from_row_tiles(xs_ref, EXPERT_ROWS)
        hb = _dot(xs.astype(BF16), wub_ref[...]) + bu_ref[...]
        glu = jnp.minimum(hb[:, :F], SWIGLU_LIMIT)
        lin = jnp.clip(hb[:, F:], -SWIGLU_LIMIT, SWIGLU_LIMIT)
        act = glu * jax.nn.sigmoid(SWIGLU_ALPHA * glu) * (lin + 1.0)
        _to_row_tiles(ys_ref, _dot(act.astype(BF16), wdb_ref[...]) + bd_ref[...])


def _experts(xs, group_expert, nb_used, layer, w_up, b_up, w_down, b_down):
    depth, E, D, F2 = w_up.shape
    F = w_down.shape[2]
    RT = EXPERT_ROWS * SUBLANES
    nb = xs.shape[0] // RT
    assert D % CAST_ROWS == 0 and F % CAST_ROWS == 0
    row_block = lambda b, ge, nbu: (jnp.minimum(b, nbu[0] - 1), 0)
    expert_block = lambda b, ge, nbu: (layer, ge[b], 0, 0)
    grid_spec = pltpu.PrefetchScalarGridSpec(
        num_scalar_prefetch=2,
        grid=(nb,),
        in_specs=[
            pl.BlockSpec((RT, LANES), row_block),
            pl.BlockSpec((None, None, D, F2), expert_block),
            pl.BlockSpec((None, None, 1, F2), expert_block),
            pl.BlockSpec((None, None, F, D), expert_block),
            pl.BlockSpec((None, None, 1, D), expert_block),
        ],
        out_specs=pl.BlockSpec((RT, LANES), lambda b, ge, nbu: (b, 0)),
        scratch_shapes=[pltpu.VMEM((D, F2), BF16), pltpu.VMEM((F, D), BF16)],
    )
    return pl.pallas_call(
        _expert_kernel,
        grid_spec=grid_spec,
        out_shape=jax.ShapeDtypeStruct(xs.shape, F32),
        compiler_params=pltpu.CompilerParams(
            dimension_semantics=("arbitrary",), vmem_limit_bytes=VMEM_LIMIT),
        name="experts",
    )(group_expert, nb_used, xs, w_up, b_up.reshape(depth, E, 1, F2), w_down,
      b_down.reshape(depth, E, 1, D))


def _moe_output(step, n_steps, dest_ref, dnext_ref, gate_ref, ys_ref, buf_ref, sems):
    TT = gate_ref.shape[0]
    slot = lax.rem(step, 2)

    def for_each_row(idx_ref, s, fn):
        def body(t, carry):
            for kk in range(TOP_K):
                d = idx_ref[0, 0, kk * TT + t]
                q = kk % DMA_QUEUES
                fn(pltpu.make_async_copy(ys_ref.at[_row_tile(d), :],
                                         buf_ref.at[s, kk, _row_tile(t), :], sems.at[s, q]), q)
            return carry
        lax.fori_loop(0, TT, body, 0)

    @pl.when(step == 0)
    def _():
        for_each_row(dest_ref, 0, lambda cp, q: cp.start(priority=q))

    @pl.when(step + 1 < n_steps)
    def _():
        for_each_row(dnext_ref, 1 - slot, lambda cp, q: cp.start(priority=q))

    for_each_row(dest_ref, slot, lambda cp, q: cp.wait())
    f = gate_ref[:, 0:1] * _from_row_tiles(buf_ref.at[slot, 0], TT)
    for kk in range(1, TOP_K):
        f = f + gate_ref[:, kk:kk + 1] * _from_row_tiles(buf_ref.at[slot, kk], TT)
    return f


def _moe_gather_specs(tile, n_tiles, step_of):
    slots = (1, 1, TOP_K * tile)
    in_specs = [
        pl.BlockSpec(slots, lambda *g: (step_of(*g), 0, 0), memory_space=pltpu.SMEM),
        pl.BlockSpec(slots, lambda *g: (jnp.minimum(step_of(*g) + 1, n_tiles - 1), 0, 0),
                     memory_space=pltpu.SMEM),
        pl.BlockSpec((tile, TOP_K), lambda *g: (step_of(*g), 0)),
        pl.BlockSpec(memory_space=pl.ANY),
    ]
    scratch = [pltpu.VMEM((2, TOP_K, tile * SUBLANES, LANES), F32),
               pltpu.SemaphoreType.DMA((2, DMA_QUEUES))]
    return in_specs, scratch


def _combine_kernel(dest_ref, dnext_ref, gate_ref, ys_ref, x_ref, lng_ref, lnb_ref, out_ref,
                    buf_ref, sems, *, alpha):
    f = _moe_output(pl.program_id(0), pl.num_programs(0), dest_ref, dnext_ref, gate_ref, ys_ref,
                    buf_ref, sems)
    out_ref[...] = _layer_norm(alpha * x_ref[...] + f, lng_ref[...], lnb_ref[...])


def _combine(xt, moe_out, ln_g, ln_b, alpha):
    ys, dest, gates_col = moe_out
    T, D = xt.shape
    nt = T // COMBINE_TILE
    dest_tiles = _tile_slots(dest, COMBINE_TILE)
    gather_specs, gather_scratch = _moe_gather_specs(COMBINE_TILE, nt, lambda i: i)
    kern = functools.partial(_combine_kernel, alpha=alpha)
    return pl.pallas_call(
        kern,
        grid=(nt,),
        in_specs=gather_specs + [
            pl.BlockSpec((COMBINE_TILE, D), lambda i: (i, 0)),
            pl.BlockSpec((1, D), lambda i: (0, 0)),
            pl.BlockSpec((1, D), lambda i: (0, 0)),
        ],
        out_specs=pl.BlockSpec((COMBINE_TILE, D), lambda i: (i, 0)),
        out_shape=jax.ShapeDtypeStruct((T, D), F32),
        scratch_shapes=gather_scratch,
        compiler_params=pltpu.CompilerParams(
            dimension_semantics=("arbitrary",), vmem_limit_bytes=VMEM_LIMIT),
        name="combine",
    )(dest_tiles, dest_tiles, gates_col, ys, xt, ln_g.reshape(1, D), ln_b.reshape(1, D))


def _tile_slots(dest, tile):
    K, T = dest.shape
    return dest.reshape(K, T // tile, tile).transpose(1, 0, 2).reshape(T // tile, 1, K * tile)


def _moe_experts(xt, layer, router_w, router_b, w_up, b_up, w_down, b_down):
    T, D = xt.shape
    E = router_w.shape[1]
    R = EXPERT_ROWS
    idx, gates, rank, cnt = _router(xt, router_w, router_b)
    counts = cnt[:, 0]
    padded = ((counts + R - 1) // R) * R
    pends = jnp.cumsum(padded)
    pstarts = pends - padded
    e_ids = jnp.arange(E, dtype=jnp.int32)[:, None, None]
    dest = rank + jnp.sum(jnp.where(idx[None] == e_ids, pstarts[:, None, None], 0), axis=0)
    n_rows = -(-(T * TOP_K + E * (R - 1)) // R) * R
    nb = n_rows // R
    nb_used = (pends[E - 1:] // R).astype(jnp.int32)
    blk = jnp.minimum(jnp.arange(nb, dtype=jnp.int32), nb_used - 1) * R
    group_expert = jnp.sum((pends[None, :] <= blk[:, None]).astype(jnp.int32), axis=1)
    xs = _dispatch(xt, _tile_slots(dest, DISPATCH_TILE), (pstarts + counts).astype(jnp.int32),
                   (padded - counts).astype(jnp.int32), nb_used, n_rows)
    ys = _experts(xs, group_expert, nb_used, layer, w_up, b_up, w_down, b_down)
    return ys, dest, gates.T


def kernel(x, ab_w_in, ab_b_igate, ab_b_fgate, ab_conv_w, ab_head_gain, ab_w_out, cf_w_pw1, cf_w_dw,
           cf_b_dw, cf_ln_g, cf_ln_b, cf_w_pw2, router_w, router_b, exp_w_up, exp_b_up, exp_w_down,
           exp_b_down, post_ln_g, post_ln_b):
    Bn, S, D = x.shape
    depth = post_ln_g.shape[0]
    alpha = float((2 * depth) ** 0.25)
    moe_out = None
    for layer in range(depth):
        j = layer // 2
        if layer % 2 == 0:
            if moe_out is not None:
                x = _combine(x.reshape(Bn * S, D), moe_out, post_ln_g[layer - 1, 1],
                             post_ln_b[layer - 1, 1], alpha).reshape(Bn, S, D)
            x = _mixer_ab(x, ab_w_in[j], ab_b_igate[j], ab_b_fgate[j], ab_conv_w[j], ab_head_gain[j],
                          ab_w_out[j], post_ln_g[layer, 0], post_ln_b[layer, 0], alpha)
        else:
            x = _conformer(x, moe_out, post_ln_g[layer - 1, 1], post_ln_b[layer - 1, 1], cf_w_pw1[j],
                           cf_w_dw[j], cf_b_dw[j], cf_ln_g[j], cf_ln_b[j], cf_w_pw2[j],
                           post_ln_g[layer, 0], post_ln_b[layer, 0], alpha)
        moe_out = _moe_experts(x.reshape(Bn * S, D), layer, router_w[layer], router_b[layer],
                               exp_w_up, exp_b_up, exp_w_down, exp_b_down)
    x = _combine(x.reshape(Bn * S, D), moe_out, post_ln_g[depth - 1, 1], post_ln_b[depth - 1, 1],
                 alpha)
    return x.reshape(Bn, S, D)
```

```python
import functools

import jax
import jax.numpy as jnp
from jax import lax
from jax.experimental import pallas as pl
from jax.experimental.pallas import tpu as pltpu

F32 = jnp.float32
BF16 = jnp.bfloat16

LANES = 128
SUBLANES = 8

MLSTM_HEADS = 4
MLSTM_QK_DIM = 64
MLSTM_V_DIM = 128
TOP_K = 4
SWIGLU_LIMIT = 7.0
SWIGLU_ALPHA = 1.702
LN_EPS = 1e-5
HEAD_NORM_EPS = 1e-6

SEQ_TILE = 512
MLSTM_CHUNK = 128
ROUTER_TILE = 512
DISPATCH_TILE = 256
COMBINE_TILE = 256
EXPERT_ROWS = 512
CONV_ROWS = 64
CAST_ROWS = 128
DMA_QUEUES = 2
VMEM_LIMIT = 56 * 1024 * 1024


def _layer_norm(y, g, b):
    mu = jnp.mean(y, axis=-1, keepdims=True)
    yc = y - mu
    var = jnp.mean(yc * yc, axis=-1, keepdims=True)
    return yc * lax.rsqrt(var + LN_EPS) * g + b


def _log_sigmoid(x):
    return jnp.minimum(x, 0.0) - jnp.log1p(jnp.exp(-jnp.abs(x)))


def _dot(a, b):
    return jnp.dot(a, b, preferred_element_type=F32)


def _dot_nt(a, b):
    return lax.dot_general(a, b, (((1,), (1,)), ((), ())), preferred_element_type=F32)


def _dot_tn(a, b):
    return lax.dot_general(a, b, (((0,), (0,)), ((), ())), preferred_element_type=F32)


def _dot_f32(a, b):
    return jnp.dot(a, b, preferred_element_type=F32, precision=lax.Precision.HIGHEST)


def _mixer_ab_kernel(x_ref, wall_ref, wift_ref, bcol_ref, brow_ref, convw_ref, gain_ref, wout_ref,
                     lng_ref, lnb_ref, out_ref,
                     z_ref, hy_ref, ubuf_ref, c_ref, n_ref, m_ref, *, alpha, sconv_w):
    H, L, TS = MLSTM_HEADS, MLSTM_CHUNK, SEQ_TILE
    HW = H * LANES
    q0, k0, v0, o0 = 0, HW, 2 * HW, 3 * HW
    sb0 = 4 * HW
    sc0 = sb0 + sconv_w
    sx0 = sc0 + sconv_w
    if0 = sx0 + sconv_w

    @pl.when(pl.program_id(1) == 0)
    def _():
        c_ref[...] = jnp.zeros_like(c_ref)
        n_ref[...] = jnp.zeros_like(n_ref)
        m_ref[...] = jnp.zeros_like(m_ref)
        ubuf_ref[0:SUBLANES, :] = jnp.zeros((SUBLANES, sconv_w), F32)

    x = x_ref[...]
    xb = x.astype(BF16)
    z_ref[...] = _dot(xb, wall_ref[...])
    pre_row = _dot_nt(wift_ref[...], xb) + brow_ref[...]
    logf_row = _log_sigmoid(pre_row)
    pre_col = z_ref[:, if0:if0 + LANES] + bcol_ref[...]
    logf_col = _log_sigmoid(pre_col)

    r_i = lax.broadcasted_iota(jnp.int32, (L, L), 0)
    c_i = lax.broadcasted_iota(jnp.int32, (L, L), 1)
    causal = r_i >= c_i
    tril = causal.astype(F32)
    triu = (r_i <= c_i).astype(F32)

    for c in range(TS // L):
        r0 = c * L
        b_col = _dot_f32(tril, logf_col[r0:r0 + L, :])
        b_row = _dot_f32(logf_row[:, r0:r0 + L], triu)
        for h in range(H):
            q = z_ref[r0:r0 + L, q0 + h * LANES:q0 + (h + 1) * LANES]
            k = z_ref[r0:r0 + L, k0 + h * LANES:k0 + (h + 1) * LANES]
            v = z_ref[r0:r0 + L, v0 + h * LANES:v0 + (h + 1) * LANES]
            o = z_ref[r0:r0 + L, o0 + h * LANES:o0 + (h + 1) * LANES]
            qb, kb, vb = q.astype(BF16), k.astype(BF16), v.astype(BF16)
            bc = b_col[:, H + h:H + h + 1]
            ic = pre_col[r0:r0 + L, h:h + 1]
            br = b_row[H + h:H + h + 1, :]
            ir = pre_row[h:h + 1, r0:r0 + L]
            c_prev = c_ref[h]
            n_prev = n_ref[h]
            m_prev = m_ref[h][:, 0:1]

            dmat = jnp.where(causal, bc - br + ir, -jnp.inf)
            e_inter = bc + m_prev
            m_t = jnp.maximum(e_inter, jnp.max(dmat, axis=-1, keepdims=True))
            w_intra = jnp.exp(dmat - m_t)
            w_inter = jnp.exp(e_inter - m_t)
            s = _dot_nt(qb, kb) * w_intra
            num = w_inter * _dot(qb, c_prev.astype(BF16)) + _dot(s.astype(BF16), vb)
            den = (w_inter * jnp.sum(q * n_prev, axis=-1, keepdims=True)
                   + jnp.sum(s, axis=-1, keepdims=True))
            hh = num / jnp.maximum(jnp.abs(den), jnp.exp(-m_t))

            g = bc[L - 1:L, :]
            a_col = g - bc + ic
            m_new = jnp.maximum(g + m_prev, jnp.max(a_col, axis=0, keepdims=True))
            decay = jnp.exp(g + m_prev - m_new)
            kw = k * jnp.exp(a_col - m_new)
            c_ref[h] = decay * c_prev + _dot_tn(kw.astype(BF16), vb)
            n_ref[h] = decay * n_prev + jnp.sum(kw, axis=0, keepdims=True)
            m_ref[h] = jnp.broadcast_to(m_new, (1, LANES))

            mu = jnp.mean(hh, axis=-1, keepdims=True)
            hc = hh - mu
            var = jnp.mean(hc * hc, axis=-1, keepdims=True)
            hn = hc * lax.rsqrt(var + HEAD_NORM_EPS) * gain_ref[:, h * LANES:(h + 1) * LANES]
            hy_ref[r0:r0 + L, h * LANES:(h + 1) * LANES] = (jax.nn.sigmoid(o) * hn).astype(BF16)

    u = z_ref[:, sc0:sc0 + sconv_w] * z_ref[:, sx0:sx0 + sconv_w]
    ubuf_ref[SUBLANES:SUBLANES + TS, :] = u
    conv = (convw_ref[2:3, :] * u
            + convw_ref[1:2, :] * ubuf_ref[SUBLANES - 1:SUBLANES - 1 + TS, :]
            + convw_ref[0:1, :] * ubuf_ref[SUBLANES - 2:SUBLANES - 2 + TS, :])
    hy_ref[:, HW:HW + sconv_w] = (z_ref[:, sb0:sb0 + sconv_w] * conv).astype(BF16)
    ubuf_ref[0:SUBLANES, :] = ubuf_ref[TS:TS + SUBLANES, :]

    mix = _dot(hy_ref[...], wout_ref[...])
    out_ref[...] = _layer_norm(alpha * x + mix, lng_ref[...], lnb_ref[...])


def _mixer_ab(x, w_in, b_igate, b_fgate, conv_w, head_gain, w_out, ln_g, ln_b, alpha):
    Bn, S, D = x.shape
    H, dk, dv = MLSTM_HEADS, MLSTM_QK_DIM, MLSTM_V_DIM
    assert dv == LANES and dk <= LANES and S % SEQ_TILE == 0 and SEQ_TILE % MLSTM_CHUNK == 0
    hw = H * dv
    sw = D - hw
    cuts = [H * dk, 2 * H * dk, 2 * H * dk + hw, 2 * H * dk + 2 * hw]
    wq, wk, wv, wo = (w_in[:, a:b] for a, b in zip([0] + cuts[:-1], cuts))
    wi = w_in[:, cuts[-1]:cuts[-1] + H]
    wf = w_in[:, cuts[-1] + H:cuts[-1] + 2 * H]
    ws = w_in[:, cuts[-1] + 2 * H:]

    def pad_heads(w, scale):
        w = (w * scale).reshape(D, H, dk)
        return jnp.pad(w, ((0, 0), (0, 0), (0, LANES - dk))).reshape(D, H * LANES)

    w_if = jnp.concatenate([wi, wf], axis=1)
    w_all = jnp.concatenate(
        [pad_heads(wq, 1.0), pad_heads(wk, dk ** -0.5), wv, wo, ws,
         jnp.pad(w_if, ((0, 0), (0, LANES - 2 * H)))], axis=1).astype(BF16)
    nz = w_all.shape[1]
    w_if_t = w_if.T.astype(BF16)
    b_if = jnp.concatenate([b_igate, b_fgate]).astype(F32)
    b_col = jnp.pad(b_if, (0, LANES - 2 * H)).reshape(1, LANES)
    b_row = b_if.reshape(2 * H, 1)

    const = lambda b, j: (0, 0)
    kern = functools.partial(_mixer_ab_kernel, alpha=alpha, sconv_w=sw)
    return pl.pallas_call(
        kern,
        grid=(Bn, S // SEQ_TILE),
        in_specs=[
            pl.BlockSpec((None, SEQ_TILE, D), lambda b, j: (b, j, 0)),
            pl.BlockSpec((D, nz), const),
            pl.BlockSpec((2 * H, D), const),
            pl.BlockSpec((1, LANES), const),
            pl.BlockSpec((2 * H, 1), const),
            pl.BlockSpec((conv_w.shape[0], sw), const),
            pl.BlockSpec((1, hw), const),
            pl.BlockSpec((D, D), const),
            pl.BlockSpec((1, D), const),
            pl.BlockSpec((1, D), const),
        ],
        out_specs=pl.BlockSpec((None, SEQ_TILE, D), lambda b, j: (b, j, 0)),
        out_shape=jax.ShapeDtypeStruct((Bn, S, D), F32),
        scratch_shapes=[
            pltpu.VMEM((SEQ_TILE, nz), F32),
            pltpu.VMEM((SEQ_TILE, D), BF16),
            pltpu.VMEM((SEQ_TILE + 2 * SUBLANES, sw), F32),
            pltpu.VMEM((H, LANES, dv), F32),
            pltpu.VMEM((H, 1, LANES), F32),
            pltpu.VMEM((H, 1, LANES), F32),
        ],
        compiler_params=pltpu.CompilerParams(
            dimension_semantics=("parallel", "arbitrary"), vmem_limit_bytes=VMEM_LIMIT),
        name="mixer_ab",
    )(x, w_all, w_if_t, b_col, b_row, conv_w.astype(F32), head_gain.reshape(1, hw).astype(F32),
      w_out.astype(BF16), ln_g.reshape(1, D), ln_b.reshape(1, D))


def _conformer_kernel(dest_ref, dnext_ref, gate_ref, ys_ref, x_ref, mlng_ref, mlnb_ref,
                      w1_ref, wdw_ref, bdw_ref, cg_ref, cb_ref, w2_ref, lng_ref, lnb_ref,
                      out_ref, ubuf_ref, uc_ref, sh_ref, buf_ref, sems, *, alpha, taps, hist):
    TS = SEQ_TILE
    C = uc_ref.shape[1]

    @pl.when(pl.program_id(1) == 0)
    def _():
        ubuf_ref[0:hist, :] = jnp.zeros((hist, C), F32)

    step = pl.program_id(0) * pl.num_programs(1) + pl.program_id(1)
    f = _moe_output(step, pl.num_programs(0) * pl.num_programs(1), dest_ref, dnext_ref, gate_ref,
                    ys_ref, buf_ref, sems)
    x = _layer_norm(alpha * x_ref[...] + f, mlng_ref[...], mlnb_ref[...])
    z = _dot(x.astype(BF16), w1_ref[...])
    ubuf_ref[hist:hist + TS, :] = z[:, :C] * jax.nn.sigmoid(z[:, C:])

    base = hist - (taps - 1)
    sh_rows = sh_ref.shape[1]
    for cb in range(C // LANES):
        cs = slice(cb * LANES, (cb + 1) * LANES)
        for r in range(1, SUBLANES):
            sh_ref[r - 1] = ubuf_ref[r:r + sh_rows, cs]
        for rb in range(TS // CONV_ROWS):
            acc = jnp.zeros((CONV_ROWS, LANES), F32) + bdw_ref[:, cs]
            for j in range(taps):
                r = (base + j) % SUBLANES
                start = rb * CONV_ROWS + base + j - r
                if r == 0:
                    win = ubuf_ref[start:start + CONV_ROWS, cs]
                else:
                    win = sh_ref[r - 1, start:start + CONV_ROWS, :]
                acc = acc + wdw_ref[j:j + 1, cs] * win
            uc_ref[rb * CONV_ROWS:(rb + 1) * CONV_ROWS, cs] = acc
    ubuf_ref[0:hist, :] = ubuf_ref[TS:TS + hist, :]

    un = _layer_norm(uc_ref[...], cg_ref[...], cb_ref[...])
    act = un * jax.nn.sigmoid(un)
    mix = _dot(act.astype(BF16), w2_ref[...])
    out_ref[...] = _layer_norm(alpha * x + mix, lng_ref[...], lnb_ref[...])


def _conformer(x, moe_out, moe_ln_g, moe_ln_b, w_pw1, w_dw, b_dw, cg, cb, w_pw2, ln_g, ln_b, alpha):
    ys, dest, gates_col = moe_out
    Bn, S, D = x.shape
    taps, C = w_dw.shape
    hist = -(-(taps - 1) // SUBLANES) * SUBLANES
    assert S % SEQ_TILE == 0 and C % LANES == 0 and hist <= SEQ_TILE
    nj = S // SEQ_TILE
    const = lambda b, j: (0, 0)
    dest_tiles = _tile_slots(dest, SEQ_TILE)
    gather_specs, gather_scratch = _moe_gather_specs(SEQ_TILE, Bn * nj, lambda b, j: b * nj + j)
    kern = functools.partial(_conformer_kernel, alpha=alpha, taps=taps, hist=hist)
    return pl.pallas_call(
        kern,
        grid=(Bn, nj),
        in_specs=gather_specs + [
            pl.BlockSpec((None, SEQ_TILE, D), lambda b, j: (b, j, 0)),
            pl.BlockSpec((1, D), const),
            pl.BlockSpec((1, D), const),
            pl.BlockSpec((D, 2 * C), const),
            pl.BlockSpec((taps, C), const),
            pl.BlockSpec((1, C), const),
            pl.BlockSpec((1, C), const),
            pl.BlockSpec((1, C), const),
            pl.BlockSpec((C, D), const),
            pl.BlockSpec((1, D), const),
            pl.BlockSpec((1, D), const),
        ],
        out_specs=pl.BlockSpec((None, SEQ_TILE, D), lambda b, j: (b, j, 0)),
        out_shape=jax.ShapeDtypeStruct((Bn, S, D), F32),
        scratch_shapes=[
            pltpu.VMEM((SEQ_TILE + hist, C), F32),
            pltpu.VMEM((SEQ_TILE, C), F32),
            pltpu.VMEM((SUBLANES - 1, SEQ_TILE + hist - SUBLANES, LANES), F32),
        ] + gather_scratch,
        compiler_params=pltpu.CompilerParams(
            dimension_semantics=("arbitrary", "arbitrary"), vmem_limit_bytes=VMEM_LIMIT),
        name="conformer",
    )(dest_tiles, dest_tiles, gates_col, ys, x, moe_ln_g.reshape(1, D), moe_ln_b.reshape(1, D),
      w_pw1.astype(BF16), w_dw.astype(F32), b_dw.reshape(1, C), cg.reshape(1, C),
      cb.reshape(1, C), w_pw2.astype(BF16), ln_g.reshape(1, D), ln_b.reshape(1, D))


def _router_kernel(x_ref, rwt_ref, rb_ref, idx_ref, gate_ref, rank_ref, cnt_ref, base_ref):
    E = rwt_ref.shape[0]
    TT = x_ref.shape[0]

    @pl.when(pl.program_id(0) == 0)
    def _():
        base_ref[...] = jnp.zeros_like(base_ref)

    logits = lax.dot_general(rwt_ref[...], x_ref[...], (((1,), (1,)), ((), ())),
                             preferred_element_type=F32,
                             precision=lax.Precision.HIGHEST) + rb_ref[...]
    e_iota = lax.broadcasted_iota(jnp.int32, (E, TT), 0)
    work = logits
    vals, idxs, hots = [], [], []
    for _ in range(TOP_K):
        mx = jnp.max(work, axis=0, keepdims=True)
        ix = jnp.min(jnp.where(work == mx, e_iota, E), axis=0, keepdims=True)
        hot = e_iota == ix
        work = jnp.where(hot, -jnp.inf, work)
        vals.append(mx)
        idxs.append(ix)
        hots.append(hot)
    ex = [jnp.exp(v - vals[0]) for v in vals]
    tot = functools.reduce(lambda a, b: a + b, ex)

    member = functools.reduce(lambda a, b: a | b, hots).astype(F32)
    s_i = lax.broadcasted_iota(jnp.int32, (TT, TT), 0)
    t_i = lax.broadcasted_iota(jnp.int32, (TT, TT), 1)
    before = (s_i < t_i).astype(BF16)
    pos = _dot(member.astype(BF16), before) + base_ref[:, 0:1]
    for kk in range(TOP_K):
        idx_ref[kk:kk + 1, :] = idxs[kk]
        gate_ref[kk:kk + 1, :] = ex[kk] / tot
        rank_ref[kk:kk + 1, :] = jnp.sum(jnp.where(hots[kk], pos, 0.0), axis=0,
                                         keepdims=True).astype(jnp.int32)
    base_ref[...] = base_ref[...] + jnp.sum(member, axis=1, keepdims=True)
    cnt_ref[...] = base_ref[...].astype(jnp.int32)


def _router(xt, router_w, router_b):
    T, D = xt.shape
    E = router_w.shape[1]
    assert T % ROUTER_TILE == 0
    tok = lambda i: (0, i)
    return pl.pallas_call(
        _router_kernel,
        grid=(T // ROUTER_TILE,),
        in_specs=[
            pl.BlockSpec((ROUTER_TILE, D), lambda i: (i, 0)),
            pl.BlockSpec((E, D), lambda i: (0, 0)),
            pl.BlockSpec((E, 1), lambda i: (0, 0)),
        ],
        out_specs=[
            pl.BlockSpec((TOP_K, ROUTER_TILE), tok),
            pl.BlockSpec((TOP_K, ROUTER_TILE), tok),
            pl.BlockSpec((TOP_K, ROUTER_TILE), tok),
            pl.BlockSpec((E, LANES), lambda i: (0, 0)),
        ],
        out_shape=[
            jax.ShapeDtypeStruct((TOP_K, T), jnp.int32),
            jax.ShapeDtypeStruct((TOP_K, T), F32),
            jax.ShapeDtypeStruct((TOP_K, T), jnp.int32),
            jax.ShapeDtypeStruct((E, LANES), jnp.int32),
        ],
        scratch_shapes=[pltpu.VMEM((E, LANES), F32)],
        compiler_params=pltpu.CompilerParams(
            dimension_semantics=("arbitrary",), vmem_limit_bytes=VMEM_LIMIT),
        name="router",
    )(xt, router_w.T.astype(F32), router_b.reshape(E, 1).astype(F32))


def _row_tile(i):
    return pl.ds(pl.multiple_of(i * SUBLANES, SUBLANES), SUBLANES)


def _to_row_tiles(dst_ref, val):
    n = val.shape[0]
    for s in range(SUBLANES):
        dst_ref[pl.ds(s, n, stride=SUBLANES), :] = val[:, s * LANES:(s + 1) * LANES]


def _from_row_tiles(src_ref, n):
    return jnp.concatenate(
        [src_ref[pl.ds(s, n, stride=SUBLANES), :] for s in range(SUBLANES)], axis=-1)


def _dispatch_kernel(pad_start_ref, pad_cnt_ref, nbu_ref, dest_ref, x_ref, xs_ref, src_ref, zblk_ref,
                     sems):
    TT = x_ref.shape[0]
    E = pad_start_ref.shape[0]
    RT = zblk_ref.shape[0]
    nb = xs_ref.shape[0] // RT

    _to_row_tiles(src_ref, x_ref[...])

    def issue(t, carry):
        for kk in range(TOP_K):
            d = dest_ref[0, 0, kk * TT + t]
            q = kk % DMA_QUEUES
            pltpu.make_async_copy(src_ref.at[_row_tile(t), :], xs_ref.at[_row_tile(d), :],
                                  sems.at[q]).start(priority=q)
        return carry

    lax.fori_loop(0, TT, issue, 0)
    for kk in range(TOP_K):
        pltpu.make_async_copy(src_ref, xs_ref.at[pl.ds(0, TT * SUBLANES), :],
                              sems.at[kk % DMA_QUEUES]).wait()

    @pl.when(pl.program_id(0) == pl.num_programs(0) - 1)
    def _():
        zblk_ref[...] = jnp.zeros_like(zblk_ref)
        sem = sems.at[0]

        def for_each_pad(fn):
            def per_expert(e, carry):
                n = pad_cnt_ref[e]
                row = pad_start_ref[e]
                p = RT // SUBLANES // 2
                while p >= 1:
                    take = (n & p) != 0
                    at = row

                    @pl.when(take)
                    def _():
                        fn(pltpu.make_async_copy(
                            zblk_ref.at[pl.ds(0, p * SUBLANES), :],
                            xs_ref.at[pl.ds(pl.multiple_of(at * SUBLANES, SUBLANES), p * SUBLANES), :],
                            sem))

                    row = row + jnp.where(take, p, 0)
                    p //= 2
                return carry
            lax.fori_loop(0, E, per_expert, 0)

            def per_block(blk, carry):
                fn(pltpu.make_async_copy(
                    zblk_ref, xs_ref.at[pl.ds(pl.multiple_of(blk * RT, RT), RT), :], sem))
                return carry
            lax.fori_loop(nbu_ref[0], nb, per_block, 0)

        for_each_pad(lambda cp: cp.start())
        for_each_pad(lambda cp: cp.wait())


def _dispatch(xt, dest_tiles, pad_start, pad_cnt, nb_used, n_rows):
    T, D = xt.shape
    nt = T // DISPATCH_TILE
    grid_spec = pltpu.PrefetchScalarGridSpec(
        num_scalar_prefetch=3,
        grid=(nt,),
        in_specs=[
            pl.BlockSpec((1, 1, TOP_K * DISPATCH_TILE), lambda i, ps, pc, nbu: (i, 0, 0),
                         memory_space=pltpu.SMEM),
            pl.BlockSpec((DISPATCH_TILE, D), lambda i, ps, pc, nbu: (i, 0)),
        ],
        out_specs=pl.BlockSpec(memory_space=pl.ANY),
        scratch_shapes=[pltpu.VMEM((DISPATCH_TILE * SUBLANES, LANES), F32),
                        pltpu.VMEM((EXPERT_ROWS * SUBLANES, LANES), F32),
                        pltpu.SemaphoreType.DMA((DMA_QUEUES,))],
    )
    assert D == SUBLANES * LANES
    return pl.pallas_call(
        _dispatch_kernel,
        grid_spec=grid_spec,
        out_shape=jax.ShapeDtypeStruct((n_rows * SUBLANES, LANES), F32),
        compiler_params=pltpu.CompilerParams(
            dimension_semantics=("arbitrary",), vmem_limit_bytes=VMEM_LIMIT),
        name="dispatch",
    )(pad_start, pad_cnt, nb_used, dest_tiles, xt)


def _cast_rows(src_ref, dst_ref):
    rows = src_ref.shape[0]

    def body(i, carry):
        r = pl.multiple_of(i * CAST_ROWS, CAST_ROWS)
        dst_ref[pl.ds(r, CAST_ROWS), :] = src_ref[pl.ds(r, CAST_ROWS), :].astype(dst_ref.dtype)
        return carry

    lax.fori_loop(0, rows // CAST_ROWS, body, 0)


def _expert_kernel(ge_ref, nbu_ref, xs_ref, wu_ref, bu_ref, wd_ref, bd_ref, ys_ref, wub_ref, wdb_ref):
    F = wd_ref.shape[0]
    b = pl.program_id(0)

    @pl.when(b >= nbu_ref[0])
    def _():
        ys_ref[...] = jnp.zeros_like(ys_ref)

    @pl.when(b < nbu_ref[0])
    def _():
        @pl.when((b == 0) | (ge_ref[b] != ge_ref[jnp.maximum(b - 1, 0)]))
        def _():
            _cast_rows(wu_ref, wub_ref)
            _cast_rows(wd_ref, wdb_ref)

        xs = _from_row_tiles(xs_ref, EXPERT_ROWS)
        hb = _dot(xs.astype(BF16), wub_ref[...]) + bu_ref[...]
        glu = jnp.minimum(hb[:, :F], SWIGLU_LIMIT)
        lin = jnp.clip(hb[:, F:], -SWIGLU_LIMIT, SWIGLU_LIMIT)
        act = glu * jax.nn.sigmoid(SWIGLU_ALPHA * glu) * (lin + 1.0)
        _to_row_tiles(ys_ref, _dot(act.astype(BF16), wdb_ref[...]) + bd_ref[...])


def _experts(xs, group_expert, nb_used, layer, w_up, b_up, w_down, b_down):
    depth, E, D, F2 = w_up.shape
    F = w_down.shape[2]
    RT = EXPERT_ROWS * SUBLANES
    nb = xs.shape[0] // RT
    assert D % CAST_ROWS == 0 and F % CAST_ROWS == 0
    row_block = lambda b, ge, nbu: (jnp.minimum(b, nbu[0] - 1), 0)
    expert_block = lambda b, ge, nbu: (layer, ge[b], 0, 0)
    grid_spec = pltpu.PrefetchScalarGridSpec(
        num_scalar_prefetch=2,
        grid=(nb,),
        in_specs=[
            pl.BlockSpec((RT, LANES), row_block),
            pl.BlockSpec((None, None, D, F2), expert_block),
            pl.BlockSpec((None, None, 1, F2), expert_block),
            pl.BlockSpec((None, None, F, D), expert_block),
            pl.BlockSpec((None, None, 1, D), expert_block),
        ],
        out_specs=pl.BlockSpec((RT, LANES), lambda b, ge, nbu: (b, 0)),
        scratch_shapes=[pltpu.VMEM((D, F2), BF16), pltpu.VMEM((F, D), BF16)],
    )
    return pl.pallas_call(
        _expert_kernel,
        grid_spec=grid_spec,
        out_shape=jax.ShapeDtypeStruct(xs.shape, F32),
        compiler_params=pltpu.CompilerParams(
            dimension_semantics=("arbitrary",), vmem_limit_bytes=VMEM_LIMIT),
        name="experts",
    )(group_expert, nb_used, xs, w_up, b_up.reshape(depth, E, 1, F2), w_down,
      b_down.reshape(depth, E, 1, D))


def _moe_output(step, n_steps, dest_ref, dnext_ref, gate_ref, ys_ref, buf_ref, sems):
    TT = gate_ref.shape[0]
    slot = lax.rem(step, 2)

    def fetch(idx_ref, s):
        def body(t, carry):
            for kk in range(TOP_K):
                d = idx_ref[0, 0, kk * TT + t]
                q = kk % DMA_QUEUES
                pltpu.make_async_copy(ys_ref.at[_row_tile(d), :], buf_ref.at[s, kk, _row_tile(t), :],
                                      sems.at[s, q]).start(priority=q)
            return carry
        lax.fori_loop(0, TT, body, 0)

    @pl.when(step == 0)
    def _():
        fetch(dest_ref, 0)

    @pl.when(step + 1 < n_steps)
    def _():
        fetch(dnext_ref, 1 - slot)

    for kk in range(TOP_K):
        pltpu.make_async_copy(ys_ref.at[pl.ds(0, TT * SUBLANES), :], buf_ref.at[slot, kk],
                              sems.at[slot, kk % DMA_QUEUES]).wait()
    f = gate_ref[:, 0:1] * _from_row_tiles(buf_ref.at[slot, 0], TT)
    for kk in range(1, TOP_K):
        f = f + gate_ref[:, kk:kk + 1] * _from_row_tiles(buf_ref.at[slot, kk], TT)
    return f


def _moe_gather_specs(tile, n_tiles, step_of):
    slots = (1, 1, TOP_K * tile)
    in_specs = [
        pl.BlockSpec(slots, lambda *g: (step_of(*g), 0, 0), memory_space=pltpu.SMEM),
        pl.BlockSpec(slots, lambda *g: (jnp.minimum(step_of(*g) + 1, n_tiles - 1), 0, 0),
                     memory_space=pltpu.SMEM),
        pl.BlockSpec((tile, TOP_K), lambda *g: (step_of(*g), 0)),
        pl.BlockSpec(memory_space=pl.ANY),
    ]
    scratch = [pltpu.VMEM((2, TOP_K, tile * SUBLANES, LANES), F32),
               pltpu.SemaphoreType.DMA((2, DMA_QUEUES))]
    return in_specs, scratch


def _combine_kernel(dest_ref, dnext_ref, gate_ref, ys_ref, x_ref, lng_ref, lnb_ref, out_ref,
                    buf_ref, sems, *, alpha):
    f = _moe_output(pl.program_id(0), pl.num_programs(0), dest_ref, dnext_ref, gate_ref, ys_ref,
                    buf_ref, sems)
    out_ref[...] = _layer_norm(alpha * x_ref[...] + f, lng_ref[...], lnb_ref[...])


def _combine(xt, moe_out, ln_g, ln_b, alpha):
    ys, dest, gates_col = moe_out
    T, D = xt.shape
    nt = T // COMBINE_TILE
    dest_tiles = _tile_slots(dest, COMBINE_TILE)
    gather_specs, gather_scratch = _moe_gather_specs(COMBINE_TILE, nt, lambda i: i)
    kern = functools.partial(_combine_kernel, alpha=alpha)
    return pl.pallas_call(
        kern,
        grid=(nt,),
        in_specs=gather_specs + [
            pl.BlockSpec((COMBINE_TILE, D), lambda i: (i, 0)),
            pl.BlockSpec((1, D), lambda i: (0, 0)),
            pl.BlockSpec((1, D), lambda i: (0, 0)),
        ],
        out_specs=pl.BlockSpec((COMBINE_TILE, D), lambda i: (i, 0)),
        out_shape=jax.ShapeDtypeStruct((T, D), F32),
        scratch_shapes=gather_scratch,
        compiler_params=pltpu.CompilerParams(
            dimension_semantics=("arbitrary",), vmem_limit_bytes=VMEM_LIMIT),
        name="combine",
    )(dest_tiles, dest_tiles, gates_col, ys, xt, ln_g.reshape(1, D), ln_b.reshape(1, D))


def _tile_slots(dest, tile):
    K, T = dest.shape
    return dest.reshape(K, T // tile, tile).transpose(1, 0, 2).reshape(T // tile, 1, K * tile)


def _moe_experts(xt, layer, router_w, router_b, w_up, b_up, w_down, b_down):
    T, D = xt.shape
    E = router_w.shape[1]
    R = EXPERT_ROWS
    idx, gates, rank, cnt = _router(xt, router_w, router_b)
    counts = cnt[:, 0]
    padded = ((counts + R - 1) // R) * R
    pends = jnp.cumsum(padded)
    pstarts = pends - padded
    e_ids = jnp.arange(E, dtype=jnp.int32)[:, None, None]
    dest = rank + jnp.sum(jnp.where(idx[None] == e_ids, pstarts[:, None, None], 0), axis=0)
    n_rows = -(-(T * TOP_K + E * (R - 1)) // R) * R
    nb = n_rows // R
    nb_used = (pends[E - 1:] // R).astype(jnp.int32)
    blk = jnp.minimum(jnp.arange(nb, dtype=jnp.int32), nb_used - 1) * R
    group_expert = jnp.sum((pends[None, :] <= blk[:, None]).astype(jnp.int32), axis=1)
    xs = _dispatch(xt, _tile_slots(dest, DISPATCH_TILE), (pstarts + counts).astype(jnp.int32),
                   (padded - counts).astype(jnp.int32), nb_used, n_rows)
    ys = _experts(xs, group_expert, nb_used, layer, w_up, b_up, w_down, b_down)
    return ys, dest, gates.T


def kernel(x, ab_w_in, ab_b_igate, ab_b_fgate, ab_conv_w, ab_head_gain, ab_w_out, cf_w_pw1, cf_w_dw,
           cf_b_dw, cf_ln_g, cf_ln_b, cf_w_pw2, router_w, router_b, exp_w_up, exp_b_up, exp_w_down,
           exp_b_down, post_ln_g, post_ln_b):
    Bn, S, D = x.shape
    depth = post_ln_g.shape[0]
    alpha = float((2 * depth) ** 0.25)
    moe_out = None
    for layer in range(depth):
        j = layer // 2
        if layer % 2 == 0:
            if moe_out is not None:
                x = _combine(x.reshape(Bn * S, D), moe_out, post_ln_g[layer - 1, 1],
                             post_ln_b[layer - 1, 1], alpha).reshape(Bn, S, D)
            x = _mixer_ab(x, ab_w_in[j], ab_b_igate[j], ab_b_fgate[j], ab_conv_w[j], ab_head_gain[j],
                          ab_w_out[j], post_ln_g[layer, 0], post_ln_b[layer, 0], alpha)
        else:
            x = _conformer(x, moe_out, post_ln_g[layer - 1, 1], post_ln_b[layer - 1, 1], cf_w_pw1[j],
                           cf_w_dw[j], cf_b_dw[j], cf_ln_g[j], cf_ln_b[j], cf_w_pw2[j],
                           post_ln_g[layer, 0], post_ln_b[layer, 0], alpha)
        moe_out = _moe_experts(x.reshape(Bn * S, D), layer, router_w[layer], router_b[layer],
                               exp_w_up, exp_b_up, exp_w_down, exp_b_down)
    x = _combine(x.reshape(Bn * S, D), moe_out, post_ln_g[depth - 1, 1], post_ln_b[depth - 1, 1],
                 alpha)
    return x.reshape(Bn, S, D)
```

```python
import functools

import jax
import jax.numpy as jnp
from jax import lax
from jax.experimental import pallas as pl
from jax.experimental.pallas import tpu as pltpu

F32 = jnp.float32
BF16 = jnp.bfloat16

LANES = 128
SUBLANES = 8

MLSTM_HEADS = 4
MLSTM_QK_DIM = 64
MLSTM_V_DIM = 128
TOP_K = 4
SWIGLU_LIMIT = 7.0
SWIGLU_ALPHA = 1.702
LN_EPS = 1e-5
HEAD_NORM_EPS = 1e-6

SEQ_TILE = 512
MLSTM_CHUNK = 128
ROUTER_TILE = 512
DISPATCH_TILE = 256
COMBINE_TILE = 256
EXPERT_ROWS = 512
CONV_ROWS = 64
CAST_ROWS = 128
DMA_QUEUES = 2
VMEM_LIMIT = 56 * 1024 * 1024


def _layer_norm(y, g, b):
    mu = jnp.mean(y, axis=-1, keepdims=True)
    yc = y - mu
    var = jnp.mean(yc * yc, axis=-1, keepdims=True)
    return yc * lax.rsqrt(var + LN_EPS) * g + b


def _log_sigmoid(x):
    return jnp.minimum(x, 0.0) - jnp.log1p(jnp.exp(-jnp.abs(x)))


def _dot(a, b):
    return jnp.dot(a, b, preferred_element_type=F32)


def _dot_nt(a, b):
    return lax.dot_general(a, b, (((1,), (1,)), ((), ())), preferred_element_type=F32)


def _dot_tn(a, b):
    return lax.dot_general(a, b, (((0,), (0,)), ((), ())), preferred_element_type=F32)


def _dot_f32(a, b):
    return jnp.dot(a, b, preferred_element_type=F32, precision=lax.Precision.HIGHEST)


def _mixer_ab_kernel(x_ref, wall_ref, wift_ref, bcol_ref, brow_ref, convw_ref, gain_ref, wout_ref,
                     lng_ref, lnb_ref, out_ref,
                     z_ref, hy_ref, ubuf_ref, c_ref, n_ref, m_ref, *, alpha, sconv_w):
    H, L, TS = MLSTM_HEADS, MLSTM_CHUNK, SEQ_TILE
    HW = H * LANES
    q0, k0, v0, o0 = 0, HW, 2 * HW, 3 * HW
    sb0 = 4 * HW
    sc0 = sb0 + sconv_w
    sx0 = sc0 + sconv_w
    if0 = sx0 + sconv_w

    @pl.when(pl.program_id(1) == 0)
    def _():
        c_ref[...] = jnp.zeros_like(c_ref)
        n_ref[...] = jnp.zeros_like(n_ref)
        m_ref[...] = jnp.zeros_like(m_ref)
        ubuf_ref[0:SUBLANES, :] = jnp.zeros((SUBLANES, sconv_w), F32)

    x = x_ref[...]
    xb = x.astype(BF16)
    z_ref[...] = _dot(xb, wall_ref[...])
    pre_row = _dot_nt(wift_ref[...], xb) + brow_ref[...]
    logf_row = _log_sigmoid(pre_row)
    pre_col = z_ref[:, if0:if0 + LANES] + bcol_ref[...]
    logf_col = _log_sigmoid(pre_col)

    r_i = lax.broadcasted_iota(jnp.int32, (L, L), 0)
    c_i = lax.broadcasted_iota(jnp.int32, (L, L), 1)
    causal = r_i >= c_i
    tril = causal.astype(F32)
    triu = (r_i <= c_i).astype(F32)

    for c in range(TS // L):
        r0 = c * L
        b_col = _dot_f32(tril, logf_col[r0:r0 + L, :])
        b_row = _dot_f32(logf_row[:, r0:r0 + L], triu)
        for h in range(H):
            q = z_ref[r0:r0 + L, q0 + h * LANES:q0 + (h + 1) * LANES]
            k = z_ref[r0:r0 + L, k0 + h * LANES:k0 + (h + 1) * LANES]
            v = z_ref[r0:r0 + L, v0 + h * LANES:v0 + (h + 1) * LANES]
            o = z_ref[r0:r0 + L, o0 + h * LANES:o0 + (h + 1) * LANES]
            qb, kb, vb = q.astype(BF16), k.astype(BF16), v.astype(BF16)
            bc = b_col[:, H + h:H + h + 1]
            ic = pre_col[r0:r0 + L, h:h + 1]
            br = b_row[H + h:H + h + 1, :]
            ir = pre_row[h:h + 1, r0:r0 + L]
            c_prev = c_ref[h]
            n_prev = n_ref[h]
            m_prev = m_ref[h][:, 0:1]

            dmat = jnp.where(causal, bc - br + ir, -jnp.inf)
            e_inter = bc + m_prev
            m_t = jnp.maximum(e_inter, jnp.max(dmat, axis=-1, keepdims=True))
            w_intra = jnp.exp(dmat - m_t)
            w_inter = jnp.exp(e_inter - m_t)
            s = _dot_nt(qb, kb) * w_intra
            num = w_inter * _dot(qb, c_prev.astype(BF16)) + _dot(s.astype(BF16), vb)
            den = (w_inter * jnp.sum(q * n_prev, axis=-1, keepdims=True)
                   + jnp.sum(s, axis=-1, keepdims=True))
            hh = num / jnp.maximum(jnp.abs(den), jnp.exp(-m_t))

            g = bc[L - 1:L, :]
            a_col = g - bc + ic
            m_new = jnp.maximum(g + m_prev, jnp.max(a_col, axis=0, keepdims=True))
            decay = jnp.exp(g + m_prev - m_new)
            kw = k * jnp.exp(a_col - m_new)
            c_ref[h] = decay * c_prev + _dot_tn(kw.astype(BF16), vb)
            n_ref[h] = decay * n_prev + jnp.sum(kw, axis=0, keepdims=True)
            m_ref[h] = jnp.broadcast_to(m_new, (1, LANES))

            mu = jnp.mean(hh, axis=-1, keepdims=True)
            hc = hh - mu
            var = jnp.mean(hc * hc, axis=-1, keepdims=True)
            hn = hc * lax.rsqrt(var + HEAD_NORM_EPS) * gain_ref[:, h * LANES:(h + 1) * LANES]
            hy_ref[r0:r0 + L, h * LANES:(h + 1) * LANES] = (jax.nn.sigmoid(o) * hn).astype(BF16)

    u = z_ref[:, sc0:sc0 + sconv_w] * z_ref[:, sx0:sx0 + sconv_w]
    ubuf_ref[SUBLANES:SUBLANES + TS, :] = u
    conv = (convw_ref[2:3, :] * u
            + convw_ref[1:2, :] * ubuf_ref[SUBLANES - 1:SUBLANES - 1 + TS, :]
            + convw_ref[0:1, :] * ubuf_ref[SUBLANES - 2:SUBLANES - 2 + TS, :])
    hy_ref[:, HW:HW + sconv_w] = (z_ref[:, sb0:sb0 + sconv_w] * conv).astype(BF16)
    ubuf_ref[0:SUBLANES, :] = ubuf_ref[TS:TS + SUBLANES, :]

    mix = _dot(hy_ref[...], wout_ref[...])
    out_ref[...] = _layer_norm(alpha * x + mix, lng_ref[...], lnb_ref[...])


def _mixer_ab(x, w_in, b_igate, b_fgate, conv_w, head_gain, w_out, ln_g, ln_b, alpha):
    Bn, S, D = x.shape
    H, dk, dv = MLSTM_HEADS, MLSTM_QK_DIM, MLSTM_V_DIM
    assert dv == LANES and dk <= LANES and S % SEQ_TILE == 0 and SEQ_TILE % MLSTM_CHUNK == 0
    hw = H * dv
    sw = D - hw
    cuts = [H * dk, 2 * H * dk, 2 * H * dk + hw, 2 * H * dk + 2 * hw]
    wq, wk, wv, wo = (w_in[:, a:b] for a, b in zip([0] + cuts[:-1], cuts))
    wi = w_in[:, cuts[-1]:cuts[-1] + H]
    wf = w_in[:, cuts[-1] + H:cuts[-1] + 2 * H]
    ws = w_in[:, cuts[-1] + 2 * H:]

    def pad_heads(w, scale):
        w = (w * scale).reshape(D, H, dk)
        return jnp.pad(w, ((0, 0), (0, 0), (0, LANES - dk))).reshape(D, H * LANES)

    w_if = jnp.concatenate([wi, wf], axis=1)
    w_all = jnp.concatenate(
        [pad_heads(wq, 1.0), pad_heads(wk, dk ** -0.5), wv, wo, ws,
         jnp.pad(w_if, ((0, 0), (0, LANES - 2 * H)))], axis=1).astype(BF16)
    nz = w_all.shape[1]
    w_if_t = w_if.T.astype(BF16)
    b_if = jnp.concatenate([b_igate, b_fgate]).astype(F32)
    b_col = jnp.pad(b_if, (0, LANES - 2 * H)).reshape(1, LANES)
    b_row = b_if.reshape(2 * H, 1)

    const = lambda b, j: (0, 0)
    kern = functools.partial(_mixer_ab_kernel, alpha=alpha, sconv_w=sw)
    return pl.pallas_call(
        kern,
        grid=(Bn, S // SEQ_TILE),
        in_specs=[
            pl.BlockSpec((None, SEQ_TILE, D), lambda b, j: (b, j, 0)),
            pl.BlockSpec((D, nz), const),
            pl.BlockSpec((2 * H, D), const),
            pl.BlockSpec((1, LANES), const),
            pl.BlockSpec((2 * H, 1), const),
            pl.BlockSpec((conv_w.shape[0], sw), const),
            pl.BlockSpec((1, hw), const),
            pl.BlockSpec((D, D), const),
            pl.BlockSpec((1, D), const),
            pl.BlockSpec((1, D), const),
        ],
        out_specs=pl.BlockSpec((None, SEQ_TILE, D), lambda b, j: (b, j, 0)),
        out_shape=jax.ShapeDtypeStruct((Bn, S, D), F32),
        scratch_shapes=[
            pltpu.VMEM((SEQ_TILE, nz), F32),
            pltpu.VMEM((SEQ_TILE, D), BF16),
            pltpu.VMEM((SEQ_TILE + 2 * SUBLANES, sw), F32),
            pltpu.VMEM((H, LANES, dv), F32),
            pltpu.VMEM((H, 1, LANES), F32),
            pltpu.VMEM((H, 1, LANES), F32),
        ],
        compiler_params=pltpu.CompilerParams(
            dimension_semantics=("parallel", "arbitrary"), vmem_limit_bytes=VMEM_LIMIT),
        name="mixer_ab",
    )(x, w_all, w_if_t, b_col, b_row, conv_w.astype(F32), head_gain.reshape(1, hw).astype(F32),
      w_out.astype(BF16), ln_g.reshape(1, D), ln_b.reshape(1, D))


def _conformer_kernel(dest_ref, dnext_ref, gate_ref, ys_ref, x_ref, mlng_ref, mlnb_ref,
                      w1_ref, wdw_ref, bdw_ref, cg_ref, cb_ref, w2_ref, lng_ref, lnb_ref,
                      out_ref, ubuf_ref, uc_ref, sh_ref, buf_ref, sems, *, alpha, taps, hist):
    TS = SEQ_TILE
    C = uc_ref.shape[1]

    @pl.when(pl.program_id(1) == 0)
    def _():
        ubuf_ref[0:hist, :] = jnp.zeros((hist, C), F32)

    step = pl.program_id(0) * pl.num_programs(1) + pl.program_id(1)
    n_steps = pl.num_programs(0) * pl.num_programs(1)
    slot = lax.rem(step, 2)

    @pl.when(step == 0)
    def _():
        _fetch_tile_rows(dest_ref, ys_ref, buf_ref, sems, 0, TS)

    _wait_tile_rows(ys_ref, buf_ref, sems, slot, TS)
    f = _gated_sum(gate_ref, buf_ref, slot)
    x = _layer_norm(alpha * x_ref[...] + f, mlng_ref[...], mlnb_ref[...])
    z = _dot(x.astype(BF16), w1_ref[...])
    ubuf_ref[hist:hist + TS, :] = z[:, :C] * jax.nn.sigmoid(z[:, C:])

    base = hist - (taps - 1)
    sh_rows = sh_ref.shape[1]
    n_conv_blocks = (C // LANES) * (TS // CONV_ROWS)
    fetch_per_block = TS // n_conv_blocks
    assert fetch_per_block * n_conv_blocks == TS
    for cb in range(C // LANES):
        cs = slice(cb * LANES, (cb + 1) * LANES)
        for r in range(1, SUBLANES):
            sh_ref[r - 1] = ubuf_ref[r:r + sh_rows, cs]
        for rb in range(TS // CONV_ROWS):
            t0 = (cb * (TS // CONV_ROWS) + rb) * fetch_per_block
            for t in range(t0, t0 + fetch_per_block):
                _fetch_token_rows(dnext_ref, ys_ref, buf_ref, sems, 1 - slot, t, TS)
            acc = jnp.zeros((CONV_ROWS, LANES), F32) + bdw_ref[:, cs]
            for j in range(taps):
                r = (base + j) % SUBLANES
                start = rb * CONV_ROWS + base + j - r
                if r == 0:
                    win = ubuf_ref[start:start + CONV_ROWS, cs]
                else:
                    win = sh_ref[r - 1, start:start + CONV_ROWS, :]
                acc = acc + wdw_ref[j:j + 1, cs] * win
            uc_ref[rb * CONV_ROWS:(rb + 1) * CONV_ROWS, cs] = acc
    ubuf_ref[0:hist, :] = ubuf_ref[TS:TS + hist, :]

    un = _layer_norm(uc_ref[...], cg_ref[...], cb_ref[...])
    act = un * jax.nn.sigmoid(un)
    mix = _dot(act.astype(BF16), w2_ref[...])
    out_ref[...] = _layer_norm(alpha * x + mix, lng_ref[...], lnb_ref[...])

    @pl.when(step == n_steps - 1)
    def _():
        _wait_tile_rows(ys_ref, buf_ref, sems, 1 - slot, TS)


def _conformer(x, moe_out, moe_ln_g, moe_ln_b, w_pw1, w_dw, b_dw, cg, cb, w_pw2, ln_g, ln_b, alpha):
    ys, dest, gates_col = moe_out
    Bn, S, D = x.shape
    taps, C = w_dw.shape
    hist = -(-(taps - 1) // SUBLANES) * SUBLANES
    assert S % SEQ_TILE == 0 and C % LANES == 0 and hist <= SEQ_TILE
    nj = S // SEQ_TILE
    const = lambda b, j: (0, 0)
    dest_tiles = _tile_slots(dest, SEQ_TILE)
    gather_specs, gather_scratch = _moe_gather_specs(SEQ_TILE, Bn * nj, lambda b, j: b * nj + j)
    kern = functools.partial(_conformer_kernel, alpha=alpha, taps=taps, hist=hist)
    return pl.pallas_call(
        kern,
        grid=(Bn, nj),
        in_specs=gather_specs + [
            pl.BlockSpec((None, SEQ_TILE, D), lambda b, j: (b, j, 0)),
            pl.BlockSpec((1, D), const),
            pl.BlockSpec((1, D), const),
            pl.BlockSpec((D, 2 * C), const),
            pl.BlockSpec((taps, C), const),
            pl.BlockSpec((1, C), const),
            pl.BlockSpec((1, C), const),
            pl.BlockSpec((1, C), const),
            pl.BlockSpec((C, D), const),
            pl.BlockSpec((1, D), const),
            pl.BlockSpec((1, D), const),
        ],
        out_specs=pl.BlockSpec((None, SEQ_TILE, D), lambda b, j: (b, j, 0)),
        out_shape=jax.ShapeDtypeStruct((Bn, S, D), F32),
        scratch_shapes=[
            pltpu.VMEM((SEQ_TILE + hist, C), F32),
            pltpu.VMEM((SEQ_TILE, C), F32),
            pltpu.VMEM((SUBLANES - 1, SEQ_TILE + hist - SUBLANES, LANES), F32),
        ] + gather_scratch,
        compiler_params=pltpu.CompilerParams(
            dimension_semantics=("arbitrary", "arbitrary"), vmem_limit_bytes=VMEM_LIMIT),
        name="conformer",
    )(dest_tiles, dest_tiles, gates_col, ys, x, moe_ln_g.reshape(1, D), moe_ln_b.reshape(1, D),
      w_pw1.astype(BF16), w_dw.astype(F32), b_dw.reshape(1, C), cg.reshape(1, C),
      cb.reshape(1, C), w_pw2.astype(BF16), ln_g.reshape(1, D), ln_b.reshape(1, D))


def _router_kernel(x_ref, rwt_ref, rb_ref, idx_ref, gate_ref, rank_ref, cnt_ref, base_ref):
    E = rwt_ref.shape[0]
    TT = x_ref.shape[0]

    @pl.when(pl.program_id(0) == 0)
    def _():
        base_ref[...] = jnp.zeros_like(base_ref)

    logits = lax.dot_general(rwt_ref[...], x_ref[...], (((1,), (1,)), ((), ())),
                             preferred_element_type=F32,
                             precision=lax.Precision.HIGHEST) + rb_ref[...]
    e_iota = lax.broadcasted_iota(jnp.int32, (E, TT), 0)
    work = logits
    vals, idxs, hots = [], [], []
    for _ in range(TOP_K):
        mx = jnp.max(work, axis=0, keepdims=True)
        ix = jnp.min(jnp.where(work == mx, e_iota, E), axis=0, keepdims=True)
        hot = e_iota == ix
        work = jnp.where(hot, -jnp.inf, work)
        vals.append(mx)
        idxs.append(ix)
        hots.append(hot)
    ex = [jnp.exp(v - vals[0]) for v in vals]
    tot = functools.reduce(lambda a, b: a + b, ex)

    member = functools.reduce(lambda a, b: a | b, hots).astype(F32)
    s_i = lax.broadcasted_iota(jnp.int32, (TT, TT), 0)
    t_i = lax.broadcasted_iota(jnp.int32, (TT, TT), 1)
    before = (s_i < t_i).astype(BF16)
    pos = _dot(member.astype(BF16), before) + base_ref[:, 0:1]
    for kk in range(TOP_K):
        idx_ref[kk:kk + 1, :] = idxs[kk]
        gate_ref[kk:kk + 1, :] = ex[kk] / tot
        rank_ref[kk:kk + 1, :] = jnp.sum(jnp.where(hots[kk], pos, 0.0), axis=0,
                                         keepdims=True).astype(jnp.int32)
    base_ref[...] = base_ref[...] + jnp.sum(member, axis=1, keepdims=True)
    cnt_ref[...] = base_ref[...].astype(jnp.int32)


def _router(xt, router_w, router_b):
    T, D = xt.shape
    E = router_w.shape[1]
    assert T % ROUTER_TILE == 0
    tok = lambda i: (0, i)
    return pl.pallas_call(
        _router_kernel,
        grid=(T // ROUTER_TILE,),
        in_specs=[
            pl.BlockSpec((ROUTER_TILE, D), lambda i: (i, 0)),
            pl.BlockSpec((E, D), lambda i: (0, 0)),
            pl.BlockSpec((E, 1), lambda i: (0, 0)),
        ],
        out_specs=[
            pl.BlockSpec((TOP_K, ROUTER_TILE), tok),
            pl.BlockSpec((TOP_K, ROUTER_TILE), tok),
            pl.BlockSpec((TOP_K, ROUTER_TILE), tok),
            pl.BlockSpec((E, LANES), lambda i: (0, 0)),
        ],
        out_shape=[
            jax.ShapeDtypeStruct((TOP_K, T), jnp.int32),
            jax.ShapeDtypeStruct((TOP_K, T), F32),
            jax.ShapeDtypeStruct((TOP_K, T), jnp.int32),
            jax.ShapeDtypeStruct((E, LANES), jnp.int32),
        ],
        scratch_shapes=[pltpu.VMEM((E, LANES), F32)],
        compiler_params=pltpu.CompilerParams(
            dimension_semantics=("arbitrary",), vmem_limit_bytes=VMEM_LIMIT),
        name="router",
    )(xt, router_w.T.astype(F32), router_b.reshape(E, 1).astype(F32))


def _row_tile(i):
    if isinstance(i, int):
        return pl.ds(i * SUBLANES, SUBLANES)
    return pl.ds(pl.multiple_of(i * SUBLANES, SUBLANES), SUBLANES)


def _to_row_tiles(dst_ref, val):
    n = val.shape[0]
    for s in range(SUBLANES):
        dst_ref[pl.ds(s, n, stride=SUBLANES), :] = val[:, s * LANES:(s + 1) * LANES]


def _from_row_tiles(src_ref, n):
    return jnp.concatenate(
        [src_ref[pl.ds(s, n, stride=SUBLANES), :] for s in range(SUBLANES)], axis=-1)


def _dispatch_kernel(pad_start_ref, pad_cnt_ref, nbu_ref, dest_ref, x_ref, xs_ref, src_ref, zblk_ref,
                     sems):
    TT = x_ref.shape[0]
    E = pad_start_ref.shape[0]
    RT = zblk_ref.shape[0]
    nb = xs_ref.shape[0] // RT

    _to_row_tiles(src_ref, x_ref[...])

    def issue(t, carry):
        for kk in range(TOP_K):
            d = dest_ref[0, 0, kk * TT + t]
            q = kk % DMA_QUEUES
            pltpu.make_async_copy(src_ref.at[_row_tile(t), :], xs_ref.at[_row_tile(d), :],
                                  sems.at[q]).start(priority=q)
        return carry

    lax.fori_loop(0, TT, issue, 0)
    for kk in range(TOP_K):
        pltpu.make_async_copy(src_ref, xs_ref.at[pl.ds(0, TT * SUBLANES), :],
                              sems.at[kk % DMA_QUEUES]).wait()

    @pl.when(pl.program_id(0) == pl.num_programs(0) - 1)
    def _():
        zblk_ref[...] = jnp.zeros_like(zblk_ref)
        sem = sems.at[0]

        def for_each_pad(fn):
            def per_expert(e, carry):
                n = pad_cnt_ref[e]
                row = pad_start_ref[e]
                p = RT // SUBLANES // 2
                while p >= 1:
                    take = (n & p) != 0
                    at = row

                    @pl.when(take)
                    def _():
                        fn(pltpu.make_async_copy(
                            zblk_ref.at[pl.ds(0, p * SUBLANES), :],
                            xs_ref.at[pl.ds(pl.multiple_of(at * SUBLANES, SUBLANES), p * SUBLANES), :],
                            sem))

                    row = row + jnp.where(take, p, 0)
                    p //= 2
                return carry
            lax.fori_loop(0, E, per_expert, 0)

            def per_block(blk, carry):
                fn(pltpu.make_async_copy(
                    zblk_ref, xs_ref.at[pl.ds(pl.multiple_of(blk * RT, RT), RT), :], sem))
                return carry
            lax.fori_loop(nbu_ref[0], nb, per_block, 0)

        for_each_pad(lambda cp: cp.start())
        for_each_pad(lambda cp: cp.wait())


def _dispatch(xt, dest_tiles, pad_start, pad_cnt, nb_used, n_rows):
    T, D = xt.shape
    nt = T // DISPATCH_TILE
    grid_spec = pltpu.PrefetchScalarGridSpec(
        num_scalar_prefetch=3,
        grid=(nt,),
        in_specs=[
            pl.BlockSpec((1, 1, TOP_K * DISPATCH_TILE), lambda i, ps, pc, nbu: (i, 0, 0),
                         memory_space=pltpu.SMEM),
            pl.BlockSpec((DISPATCH_TILE, D), lambda i, ps, pc, nbu: (i, 0)),
        ],
        out_specs=pl.BlockSpec(memory_space=pl.ANY),
        scratch_shapes=[pltpu.VMEM((DISPATCH_TILE * SUBLANES, LANES), F32),
                        pltpu.VMEM((EXPERT_ROWS * SUBLANES, LANES), F32),
                        pltpu.SemaphoreType.DMA((DMA_QUEUES,))],
    )
    assert D == SUBLANES * LANES
    return pl.pallas_call(
        _dispatch_kernel,
        grid_spec=grid_spec,
        out_shape=jax.ShapeDtypeStruct((n_rows * SUBLANES, LANES), F32),
        compiler_params=pltpu.CompilerParams(
            dimension_semantics=("arbitrary",), vmem_limit_bytes=VMEM_LIMIT),
        name="dispatch",
    )(pad_start, pad_cnt, nb_used, dest_tiles, xt)


def _cast_rows(src_ref, dst_ref):
    rows = src_ref.shape[0]

    def body(i, carry):
        r = pl.multiple_of(i * CAST_ROWS, CAST_ROWS)
        dst_ref[pl.ds(r, CAST_ROWS), :] = src_ref[pl.ds(r, CAST_ROWS), :].astype(dst_ref.dtype)
        return carry

    lax.fori_loop(0, rows // CAST_ROWS, body, 0)


def _expert_kernel(ge_ref, nbu_ref, xs_ref, wu_ref, bu_ref, wd_ref, bd_ref, ys_ref, wub_ref, wdb_ref):
    F = wd_ref.shape[0]
    b = pl.program_id(0)

    @pl.when(b >= nbu_ref[0])
    def _():
        ys_ref[...] = jnp.zeros_like(ys_ref)

    @pl.when(b < nbu_ref[0])
    def _():
        @pl.when((b == 0) | (ge_ref[b] != ge_ref[jnp.maximum(b - 1, 0)]))
        def _():
            _cast_rows(wu_ref, wub_ref)
            _cast_rows(wd_ref, wdb_ref)

        xs = _from_row_tiles(xs_ref, EXPERT_ROWS)
        hb = _dot(xs.astype(BF16), wub_ref[...]) + bu_ref[...]
        glu = jnp.minimum(hb[:, :F], SWIGLU_LIMIT)
        lin = jnp.clip(hb[:, F:], -SWIGLU_LIMIT, SWIGLU_LIMIT)
        act = glu * jax.nn.sigmoid(SWIGLU_ALPHA * glu) * (lin + 1.0)
        _to_row_tiles(ys_ref, _dot(act.astype(BF16), wdb_ref[...]) + bd_ref[...])


def _experts(xs, group_expert, nb_used, layer, w_up, b_up, w_down, b_down):
    depth, E, D, F2 = w_up.shape
    F = w_down.shape[2]
    RT = EXPERT_ROWS * SUBLANES
    nb = xs.shape[0] // RT
    assert D % CAST_ROWS == 0 and F % CAST_ROWS == 0
    row_block = lambda b, ge, nbu: (jnp.minimum(b, nbu[0] - 1), 0)
    expert_block = lambda b, ge, nbu: (layer, ge[b], 0, 0)
    grid_spec = pltpu.PrefetchScalarGridSpec(
        num_scalar_prefetch=2,
        grid=(nb,),
        in_specs=[
            pl.BlockSpec((RT, LANES), row_block),
            pl.BlockSpec((None, None, D, F2), expert_block),
            pl.BlockSpec((None, None, 1, F2), expert_block),
            pl.BlockSpec((None, None, F, D), expert_block),
            pl.BlockSpec((None, None, 1, D), expert_block),
        ],
        out_specs=pl.BlockSpec((RT, LANES), lambda b, ge, nbu: (b, 0)),
        scratch_shapes=[pltpu.VMEM((D, F2), BF16), pltpu.VMEM((F, D), BF16)],
    )
    return pl.pallas_call(
        _expert_kernel,
        grid_spec=grid_spec,
        out_shape=jax.ShapeDtypeStruct(xs.shape, F32),
        compiler_params=pltpu.CompilerParams(
            dimension_semantics=("arbitrary",), vmem_limit_bytes=VMEM_LIMIT),
        name="experts",
    )(group_expert, nb_used, xs, w_up, b_up.reshape(depth, E, 1, F2), w_down,
      b_down.reshape(depth, E, 1, D))


def _fetch_token_rows(idx_ref, ys_ref, buf_ref, sems, s, t, TT):
    for kk in range(TOP_K):
        d = idx_ref[0, 0, kk * TT + t]
        q = kk % DMA_QUEUES
        pltpu.make_async_copy(ys_ref.at[_row_tile(d), :], buf_ref.at[s, kk, _row_tile(t), :],
                              sems.at[s, q]).start(priority=q)


def _fetch_tile_rows(idx_ref, ys_ref, buf_ref, sems, s, TT):
    def body(t, carry):
        _fetch_token_rows(idx_ref, ys_ref, buf_ref, sems, s, t, TT)
        return carry
    lax.fori_loop(0, TT, body, 0)


def _wait_tile_rows(ys_ref, buf_ref, sems, s, TT):
    for kk in range(TOP_K):
        pltpu.make_async_copy(ys_ref.at[pl.ds(0, TT * SUBLANES), :], buf_ref.at[s, kk],
                              sems.at[s, kk % DMA_QUEUES]).wait()


def _gated_sum(gate_ref, buf_ref, s):
    TT = gate_ref.shape[0]
    f = gate_ref[:, 0:1] * _from_row_tiles(buf_ref.at[s, 0], TT)
    for kk in range(1, TOP_K):
        f = f + gate_ref[:, kk:kk + 1] * _from_row_tiles(buf_ref.at[s, kk], TT)
    return f


def _moe_output(step, n_steps, dest_ref, dnext_ref, gate_ref, ys_ref, buf_ref, sems):
    TT = gate_ref.shape[0]
    slot = lax.rem(step, 2)

    @pl.when(step == 0)
    def _():
        _fetch_tile_rows(dest_ref, ys_ref, buf_ref, sems, 0, TT)

    @pl.when(step + 1 < n_steps)
    def _():
        _fetch_tile_rows(dnext_ref, ys_ref, buf_ref, sems, 1 - slot, TT)

    _wait_tile_rows(ys_ref, buf_ref, sems, slot, TT)
    return _gated_sum(gate_ref, buf_ref, slot)


def _moe_gather_specs(tile, n_tiles, step_of):
    slots = (1, 1, TOP_K * tile)
    in_specs = [
        pl.BlockSpec(slots, lambda *g: (step_of(*g), 0, 0), memory_space=pltpu.SMEM),
        pl.BlockSpec(slots, lambda *g: (jnp.minimum(step_of(*g) + 1, n_tiles - 1), 0, 0),
                     memory_space=pltpu.SMEM),
        pl.BlockSpec((tile, TOP_K), lambda *g: (step_of(*g), 0)),
        pl.BlockSpec(memory_space=pl.ANY),
    ]
    scratch = [pltpu.VMEM((2, TOP_K, tile * SUBLANES, LANES), F32),
               pltpu.SemaphoreType.DMA((2, DMA_QUEUES))]
    return in_specs, scratch


def _combine_kernel(dest_ref, dnext_ref, gate_ref, ys_ref, x_ref, lng_ref, lnb_ref, out_ref,
                    buf_ref, sems, *, alpha):
    f = _moe_output(pl.program_id(0), pl.num_programs(0), dest_ref, dnext_ref, gate_ref, ys_ref,
                    buf_ref, sems)
    out_ref[...] = _layer_norm(alpha * x_ref[...] + f, lng_ref[...], lnb_ref[...])


def _combine(xt, moe_out, ln_g, ln_b, alpha):
    ys, dest, gates_col = moe_out
    T, D = xt.shape
    nt = T // COMBINE_TILE
    dest_tiles = _tile_slots(dest, COMBINE_TILE)
    gather_specs, gather_scratch = _moe_gather_specs(COMBINE_TILE, nt, lambda i: i)
    kern = functools.partial(_combine_kernel, alpha=alpha)
    return pl.pallas_call(
        kern,
        grid=(nt,),
        in_specs=gather_specs + [
            pl.BlockSpec((COMBINE_TILE, D), lambda i: (i, 0)),
            pl.BlockSpec((1, D), lambda i: (0, 0)),
            pl.BlockSpec((1, D), lambda i: (0, 0)),
        ],
        out_specs=pl.BlockSpec((COMBINE_TILE, D), lambda i: (i, 0)),
        out_shape=jax.ShapeDtypeStruct((T, D), F32),
        scratch_shapes=gather_scratch,
        compiler_params=pltpu.CompilerParams(
            dimension_semantics=("arbitrary",), vmem_limit_bytes=VMEM_LIMIT),
        name="combine",
    )(dest_tiles, dest_tiles, gates_col, ys, xt, ln_g.reshape(1, D), ln_b.reshape(1, D))


def _tile_slots(dest, tile):
    K, T = dest.shape
    return dest.reshape(K, T // tile, tile).transpose(1, 0, 2).reshape(T // tile, 1, K * tile)


def _moe_experts(xt, layer, router_w, router_b, w_up, b_up, w_down, b_down):
    T, D = xt.shape
    E = router_w.shape[1]
    R = EXPERT_ROWS
    idx, gates, rank, cnt = _router(xt, router_w, router_b)
    counts = cnt[:, 0]
    padded = ((counts + R - 1) // R) * R
    pends = jnp.cumsum(padded)
    pstarts = pends - padded
    e_ids = jnp.arange(E, dtype=jnp.int32)[:, None, None]
    dest = rank + jnp.sum(jnp.where(idx[None] == e_ids, pstarts[:, None, None], 0), axis=0)
    n_rows = -(-(T * TOP_K + E * (R - 1)) // R) * R
    nb = n_rows // R
    nb_used = (pends[E - 1:] // R).astype(jnp.int32)
    blk = jnp.minimum(jnp.arange(nb, dtype=jnp.int32), nb_used - 1) * R
    group_expert = jnp.sum((pends[None, :] <= blk[:, None]).astype(jnp.int32), axis=1)
    xs = _dispatch(xt, _tile_slots(dest, DISPATCH_TILE), (pstarts + counts).astype(jnp.int32),
                   (padded - counts).astype(jnp.int32), nb_used, n_rows)
    ys = _experts(xs, group_expert, nb_used, layer, w_up, b_up, w_down, b_down)
    return ys, dest, gates.T


def kernel(x, ab_w_in, ab_b_igate, ab_b_fgate, ab_conv_w, ab_head_gain, ab_w_out, cf_w_pw1, cf_w_dw,
           cf_b_dw, cf_ln_g, cf_ln_b, cf_w_pw2, router_w, router_b, exp_w_up, exp_b_up, exp_w_down,
           exp_b_down, post_ln_g, post_ln_b):
    Bn, S, D = x.shape
    depth = post_ln_g.shape[0]
    alpha = float((2 * depth) ** 0.25)
    moe_out = None
    for layer in range(depth):
        j = layer // 2
        if layer % 2 == 0:
            if moe_out is not None:
                x = _combine(x.reshape(Bn * S, D), moe_out, post_ln_g[layer - 1, 1],
                             post_ln_b[layer - 1, 1], alpha).reshape(Bn, S, D)
            x = _mixer_ab(x, ab_w_in[j], ab_b_igate[j], ab_b_fgate[j], ab_conv_w[j], ab_head_gain[j],
                          ab_w_out[j], post_ln_g[layer, 0], post_ln_b[layer, 0], alpha)
        else:
            x = _conformer(x, moe_out, post_ln_g[layer - 1, 1], post_ln_b[layer - 1, 1], cf_w_pw1[j],
                           cf_w_dw[j], cf_b_dw[j], cf_ln_g[j], cf_ln_b[j], cf_w_pw2[j],
                           post_ln_g[layer, 0], post_ln_b[layer, 0], alpha)
        moe_out = _moe_experts(x.reshape(Bn * S, D), layer, router_w[layer], router_b[layer],
                               exp_w_up, exp_b_up, exp_w_down, exp_b_down)
    x = _combine(x.reshape(Bn * S, D), moe_out, post_ln_g[depth - 1, 1], post_ln_b[depth - 1, 1],
                 alpha)
    return x.reshape(Bn, S, D)
```

```python
import functools

import jax
import jax.numpy as jnp
from jax import lax
from jax.experimental import pallas as pl
from jax.experimental.pallas import tpu as pltpu

F32 = jnp.float32
BF16 = jnp.bfloat16

LANES = 128
SUBLANES = 8

MLSTM_HEADS = 4
MLSTM_QK_DIM = 64
MLSTM_V_DIM = 128
TOP_K = 4
SWIGLU_LIMIT = 7.0
SWIGLU_ALPHA = 1.702
LN_EPS = 1e-5
HEAD_NORM_EPS = 1e-6

SEQ_TILE = 512
MLSTM_CHUNK = 128
ROUTER_TILE = 512
DISPATCH_TILE = 512
COMBINE_TILE = 512
EXPERT_ROWS = 512
CONV_ROWS = 64
CAST_ROWS = 128
DMA_QUEUES = 2
ISSUE_UNROLL = 8
VMEM_LIMIT = 56 * 1024 * 1024


def _layer_norm(y, g, b):
    mu = jnp.mean(y, axis=-1, keepdims=True)
    yc = y - mu
    var = jnp.mean(yc * yc, axis=-1, keepdims=True)
    return yc * lax.rsqrt(var + LN_EPS) * g + b


def _log_sigmoid(x):
    return jnp.minimum(x, 0.0) - jnp.log1p(jnp.exp(-jnp.abs(x)))


def _dot(a, b):
    return jnp.dot(a, b, preferred_element_type=F32)


def _dot_nt(a, b):
    return lax.dot_general(a, b, (((1,), (1,)), ((), ())), preferred_element_type=F32)


def _dot_tn(a, b):
    return lax.dot_general(a, b, (((0,), (0,)), ((), ())), preferred_element_type=F32)


def _dot_f32(a, b):
    return jnp.dot(a, b, preferred_element_type=F32, precision=lax.Precision.HIGHEST)


def _mixer_ab_kernel(x_ref, wall_ref, wift_ref, bcol_ref, brow_ref, convw_ref, gain_ref, wout_ref,
                     lng_ref, lnb_ref, out_ref,
                     z_ref, hy_ref, ubuf_ref, c_ref, n_ref, m_ref, *, alpha, sconv_w):
    H, L, TS = MLSTM_HEADS, MLSTM_CHUNK, SEQ_TILE
    HW = H * LANES
    q0, k0, v0, o0 = 0, HW, 2 * HW, 3 * HW
    sb0 = 4 * HW
    sc0 = sb0 + sconv_w
    sx0 = sc0 + sconv_w
    if0 = sx0 + sconv_w

    @pl.when(pl.program_id(1) == 0)
    def _():
        c_ref[...] = jnp.zeros_like(c_ref)
        n_ref[...] = jnp.zeros_like(n_ref)
        m_ref[...] = jnp.zeros_like(m_ref)
        ubuf_ref[0:SUBLANES, :] = jnp.zeros((SUBLANES, sconv_w), F32)

    x = x_ref[...]
    xb = x.astype(BF16)
    z_ref[...] = _dot(xb, wall_ref[...])
    pre_row = _dot_nt(wift_ref[...], xb) + brow_ref[...]
    logf_row = _log_sigmoid(pre_row)
    pre_col = z_ref[:, if0:if0 + LANES] + bcol_ref[...]
    logf_col = _log_sigmoid(pre_col)

    r_i = lax.broadcasted_iota(jnp.int32, (L, L), 0)
    c_i = lax.broadcasted_iota(jnp.int32, (L, L), 1)
    causal = r_i >= c_i
    tril = causal.astype(F32)
    triu = (r_i <= c_i).astype(F32)

    for c in range(TS // L):
        r0 = c * L
        b_col = _dot_f32(tril, logf_col[r0:r0 + L, :])
        b_row = _dot_f32(logf_row[:, r0:r0 + L], triu)
        for h in range(H):
            q = z_ref[r0:r0 + L, q0 + h * LANES:q0 + (h + 1) * LANES]
            k = z_ref[r0:r0 + L, k0 + h * LANES:k0 + (h + 1) * LANES]
            v = z_ref[r0:r0 + L, v0 + h * LANES:v0 + (h + 1) * LANES]
            o = z_ref[r0:r0 + L, o0 + h * LANES:o0 + (h + 1) * LANES]
            qb, kb, vb = q.astype(BF16), k.astype(BF16), v.astype(BF16)
            bc = b_col[:, H + h:H + h + 1]
            ic = pre_col[r0:r0 + L, h:h + 1]
            br = b_row[H + h:H + h + 1, :]
            ir = pre_row[h:h + 1, r0:r0 + L]
            c_prev = c_ref[h]
            n_prev = n_ref[h]
            m_prev = m_ref[h][:, 0:1]

            dmat = jnp.where(causal, bc - br + ir, -jnp.inf)
            e_inter = bc + m_prev
            m_t = jnp.maximum(e_inter, jnp.max(dmat, axis=-1, keepdims=True))
            w_intra = jnp.exp(dmat - m_t)
            w_inter = jnp.exp(e_inter - m_t)
            s = _dot_nt(qb, kb) * w_intra
            num = w_inter * _dot(qb, c_prev.astype(BF16)) + _dot(s.astype(BF16), vb)
            den = (w_inter * jnp.sum(q * n_prev, axis=-1, keepdims=True)
                   + jnp.sum(s, axis=-1, keepdims=True))
            hh = num / jnp.maximum(jnp.abs(den), jnp.exp(-m_t))

            g = bc[L - 1:L, :]
            a_col = g - bc + ic
            m_new = jnp.maximum(g + m_prev, jnp.max(a_col, axis=0, keepdims=True))
            decay = jnp.exp(g + m_prev - m_new)
            kw = k * jnp.exp(a_col - m_new)
            c_ref[h] = decay * c_prev + _dot_tn(kw.astype(BF16), vb)
            n_ref[h] = decay * n_prev + jnp.sum(kw, axis=0, keepdims=True)
            m_ref[h] = jnp.broadcast_to(m_new, (1, LANES))

            mu = jnp.mean(hh, axis=-1, keepdims=True)
            hc = hh - mu
            var = jnp.mean(hc * hc, axis=-1, keepdims=True)
            hn = hc * lax.rsqrt(var + HEAD_NORM_EPS) * gain_ref[:, h * LANES:(h + 1) * LANES]
            hy_ref[r0:r0 + L, h * LANES:(h + 1) * LANES] = (jax.nn.sigmoid(o) * hn).astype(BF16)

    u = z_ref[:, sc0:sc0 + sconv_w] * z_ref[:, sx0:sx0 + sconv_w]
    ubuf_ref[SUBLANES:SUBLANES + TS, :] = u
    conv = (convw_ref[2:3, :] * u
            + convw_ref[1:2, :] * ubuf_ref[SUBLANES - 1:SUBLANES - 1 + TS, :]
            + convw_ref[0:1, :] * ubuf_ref[SUBLANES - 2:SUBLANES - 2 + TS, :])
    hy_ref[:, HW:HW + sconv_w] = (z_ref[:, sb0:sb0 + sconv_w] * conv).astype(BF16)
    ubuf_ref[0:SUBLANES, :] = ubuf_ref[TS:TS + SUBLANES, :]

    mix = _dot(hy_ref[...], wout_ref[...])
    out_ref[...] = _layer_norm(alpha * x + mix, lng_ref[...], lnb_ref[...])


def _mixer_ab(x, w_in, b_igate, b_fgate, conv_w, head_gain, w_out, ln_g, ln_b, alpha):
    Bn, S, D = x.shape
    H, dk, dv = MLSTM_HEADS, MLSTM_QK_DIM, MLSTM_V_DIM
    assert dv == LANES and dk <= LANES and S % SEQ_TILE == 0 and SEQ_TILE % MLSTM_CHUNK == 0
    hw = H * dv
    sw = D - hw
    cuts = [H * dk, 2 * H * dk, 2 * H * dk + hw, 2 * H * dk + 2 * hw]
    wq, wk, wv, wo = (w_in[:, a:b] for a, b in zip([0] + cuts[:-1], cuts))
    wi = w_in[:, cuts[-1]:cuts[-1] + H]
    wf = w_in[:, cuts[-1] + H:cuts[-1] + 2 * H]
    ws = w_in[:, cuts[-1] + 2 * H:]

    def pad_heads(w, scale):
        w = (w * scale).reshape(D, H, dk)
        return jnp.pad(w, ((0, 0), (0, 0), (0, LANES - dk))).reshape(D, H * LANES)

    w_if = jnp.concatenate([wi, wf], axis=1)
    w_all = jnp.concatenate(
        [pad_heads(wq, 1.0), pad_heads(wk, dk ** -0.5), wv, wo, ws,
         jnp.pad(w_if, ((0, 0), (0, LANES - 2 * H)))], axis=1).astype(BF16)
    nz = w_all.shape[1]
    w_if_t = w_if.T.astype(BF16)
    b_if = jnp.concatenate([b_igate, b_fgate]).astype(F32)
    b_col = jnp.pad(b_if, (0, LANES - 2 * H)).reshape(1, LANES)
    b_row = b_if.reshape(2 * H, 1)

    const = lambda b, j: (0, 0)
    kern = functools.partial(_mixer_ab_kernel, alpha=alpha, sconv_w=sw)
    return pl.pallas_call(
        kern,
        grid=(Bn, S // SEQ_TILE),
        in_specs=[
            pl.BlockSpec((None, SEQ_TILE, D), lambda b, j: (b, j, 0)),
            pl.BlockSpec((D, nz), const),
            pl.BlockSpec((2 * H, D), const),
            pl.BlockSpec((1, LANES), const),
            pl.BlockSpec((2 * H, 1), const),
            pl.BlockSpec((conv_w.shape[0], sw), const),
            pl.BlockSpec((1, hw), const),
            pl.BlockSpec((D, D), const),
            pl.BlockSpec((1, D), const),
            pl.BlockSpec((1, D), const),
        ],
        out_specs=pl.BlockSpec((None, SEQ_TILE, D), lambda b, j: (b, j, 0)),
        out_shape=jax.ShapeDtypeStruct((Bn, S, D), F32),
        scratch_shapes=[
            pltpu.VMEM((SEQ_TILE, nz), F32),
            pltpu.VMEM((SEQ_TILE, D), BF16),
            pltpu.VMEM((SEQ_TILE + 2 * SUBLANES, sw), F32),
            pltpu.VMEM((H, LANES, dv), F32),
            pltpu.VMEM((H, 1, LANES), F32),
            pltpu.VMEM((H, 1, LANES), F32),
        ],
        compiler_params=pltpu.CompilerParams(
            dimension_semantics=("parallel", "arbitrary"), vmem_limit_bytes=VMEM_LIMIT),
        name="mixer_ab",
    )(x, w_all, w_if_t, b_col, b_row, conv_w.astype(F32), head_gain.reshape(1, hw).astype(F32),
      w_out.astype(BF16), ln_g.reshape(1, D), ln_b.reshape(1, D))


def _conformer_kernel(dest_ref, dnext_ref, gate_ref, ys_ref, x_ref, mlng_ref, mlnb_ref,
                      w1_ref, wdw_ref, bdw_ref, cg_ref, cb_ref, w2_ref, lng_ref, lnb_ref,
                      out_ref, ubuf_ref, uc_ref, sh_ref, buf_ref, sems, *, alpha, taps, hist):
    TS = SEQ_TILE
    C = uc_ref.shape[1]

    @pl.when(pl.program_id(1) == 0)
    def _():
        ubuf_ref[0:hist, :] = jnp.zeros((hist, C), F32)

    step = pl.program_id(0) * pl.num_programs(1) + pl.program_id(1)
    n_steps = pl.num_programs(0) * pl.num_programs(1)
    slot = lax.rem(step, 2)

    @pl.when(step == 0)
    def _():
        _fetch_tile_rows(dest_ref, ys_ref, buf_ref, sems, 0, TS)

    _wait_tile_rows(ys_ref, buf_ref, sems, slot, TS)
    f = _gated_sum(gate_ref, buf_ref, slot)
    x = _layer_norm(alpha * x_ref[...] + f, mlng_ref[...], mlnb_ref[...])
    z = _dot(x.astype(BF16), w1_ref[...])
    ubuf_ref[hist:hist + TS, :] = z[:, :C] * jax.nn.sigmoid(z[:, C:])

    base = hist - (taps - 1)
    sh_rows = sh_ref.shape[1]
    n_conv_blocks = (C // LANES) * (TS // CONV_ROWS)
    fetch_per_block = TS // n_conv_blocks
    assert fetch_per_block * n_conv_blocks == TS
    for cb in range(C // LANES):
        cs = slice(cb * LANES, (cb + 1) * LANES)
        for r in range(1, SUBLANES):
            sh_ref[r - 1] = ubuf_ref[r:r + sh_rows, cs]
        for rb in range(TS // CONV_ROWS):
            t0 = (cb * (TS // CONV_ROWS) + rb) * fetch_per_block
            for t in range(t0, t0 + fetch_per_block):
                _fetch_token_rows(dnext_ref, ys_ref, buf_ref, sems, 1 - slot, t, TS)
            acc = jnp.zeros((CONV_ROWS, LANES), F32) + bdw_ref[:, cs]
            for j in range(taps):
                r = (base + j) % SUBLANES
                start = rb * CONV_ROWS + base + j - r
                if r == 0:
                    win = ubuf_ref[start:start + CONV_ROWS, cs]
                else:
                    win = sh_ref[r - 1, start:start + CONV_ROWS, :]
                acc = acc + wdw_ref[j:j + 1, cs] * win
            uc_ref[rb * CONV_ROWS:(rb + 1) * CONV_ROWS, cs] = acc
    ubuf_ref[0:hist, :] = ubuf_ref[TS:TS + hist, :]

    un = _layer_norm(uc_ref[...], cg_ref[...], cb_ref[...])
    act = un * jax.nn.sigmoid(un)
    mix = _dot(act.astype(BF16), w2_ref[...])
    out_ref[...] = _layer_norm(alpha * x + mix, lng_ref[...], lnb_ref[...])

    @pl.when(step == n_steps - 1)
    def _():
        _wait_tile_rows(ys_ref, buf_ref, sems, 1 - slot, TS)


def _conformer(x, moe_out, moe_ln_g, moe_ln_b, w_pw1, w_dw, b_dw, cg, cb, w_pw2, ln_g, ln_b, alpha):
    ys, dest, gates_col = moe_out
    Bn, S, D = x.shape
    taps, C = w_dw.shape
    hist = -(-(taps - 1) // SUBLANES) * SUBLANES
    assert S % SEQ_TILE == 0 and C % LANES == 0 and hist <= SEQ_TILE
    nj = S // SEQ_TILE
    const = lambda b, j: (0, 0)
    dest_tiles = _tile_slots(dest, SEQ_TILE)
    gather_specs, gather_scratch = _moe_gather_specs(SEQ_TILE, Bn * nj, lambda b, j: b * nj + j)
    kern = functools.partial(_conformer_kernel, alpha=alpha, taps=taps, hist=hist)
    return pl.pallas_call(
        kern,
        grid=(Bn, nj),
        in_specs=gather_specs + [
            pl.BlockSpec((None, SEQ_TILE, D), lambda b, j: (b, j, 0)),
            pl.BlockSpec((1, D), const),
            pl.BlockSpec((1, D), const),
            pl.BlockSpec((D, 2 * C), const),
            pl.BlockSpec((taps, C), const),
            pl.BlockSpec((1, C), const),
            pl.BlockSpec((1, C), const),
            pl.BlockSpec((1, C), const),
            pl.BlockSpec((C, D), const),
            pl.BlockSpec((1, D), const),
            pl.BlockSpec((1, D), const),
        ],
        out_specs=pl.BlockSpec((None, SEQ_TILE, D), lambda b, j: (b, j, 0)),
        out_shape=jax.ShapeDtypeStruct((Bn, S, D), F32),
        scratch_shapes=[
            pltpu.VMEM((SEQ_TILE + hist, C), F32),
            pltpu.VMEM((SEQ_TILE, C), F32),
            pltpu.VMEM((SUBLANES - 1, SEQ_TILE + hist - SUBLANES, LANES), F32),
        ] + gather_scratch,
        compiler_params=pltpu.CompilerParams(
            dimension_semantics=("arbitrary", "arbitrary"), vmem_limit_bytes=VMEM_LIMIT),
        name="conformer",
    )(dest_tiles, dest_tiles, gates_col, ys, x, moe_ln_g.reshape(1, D), moe_ln_b.reshape(1, D),
      w_pw1.astype(BF16), w_dw.astype(F32), b_dw.reshape(1, C), cg.reshape(1, C),
      cb.reshape(1, C), w_pw2.astype(BF16), ln_g.reshape(1, D), ln_b.reshape(1, D))


def _router_kernel(x_ref, rwt_ref, rb_ref, idx_ref, gate_ref, rank_ref, cnt_ref, base_ref):
    E = rwt_ref.shape[0]
    TT = x_ref.shape[0]

    @pl.when(pl.program_id(0) == 0)
    def _():
        base_ref[...] = jnp.zeros_like(base_ref)

    logits = lax.dot_general(rwt_ref[...], x_ref[...], (((1,), (1,)), ((), ())),
                             preferred_element_type=F32,
                             precision=lax.Precision.HIGHEST) + rb_ref[...]
    e_iota = lax.broadcasted_iota(jnp.int32, (E, TT), 0)
    work = logits
    vals, idxs, hots = [], [], []
    for _ in range(TOP_K):
        mx = jnp.max(work, axis=0, keepdims=True)
        ix = jnp.min(jnp.where(work == mx, e_iota, E), axis=0, keepdims=True)
        hot = e_iota == ix
        work = jnp.where(hot, -jnp.inf, work)
        vals.append(mx)
        idxs.append(ix)
        hots.append(hot)
    ex = [jnp.exp(v - vals[0]) for v in vals]
    tot = functools.reduce(lambda a, b: a + b, ex)

    member = functools.reduce(lambda a, b: a | b, hots).astype(F32)
    s_i = lax.broadcasted_iota(jnp.int32, (TT, TT), 0)
    t_i = lax.broadcasted_iota(jnp.int32, (TT, TT), 1)
    before = (s_i < t_i).astype(BF16)
    pos = _dot(member.astype(BF16), before) + base_ref[:, 0:1]
    for kk in range(TOP_K):
        idx_ref[kk:kk + 1, :] = idxs[kk]
        gate_ref[kk:kk + 1, :] = ex[kk] / tot
        rank_ref[kk:kk + 1, :] = jnp.sum(jnp.where(hots[kk], pos, 0.0), axis=0,
                                         keepdims=True).astype(jnp.int32)
    base_ref[...] = base_ref[...] + jnp.sum(member, axis=1, keepdims=True)
    cnt_ref[...] = base_ref[...].astype(jnp.int32)


def _router(xt, router_w, router_b):
    T, D = xt.shape
    E = router_w.shape[1]
    assert T % ROUTER_TILE == 0
    tok = lambda i: (0, i)
    return pl.pallas_call(
        _router_kernel,
        grid=(T // ROUTER_TILE,),
        in_specs=[
            pl.BlockSpec((ROUTER_TILE, D), lambda i: (i, 0)),
            pl.BlockSpec((E, D), lambda i: (0, 0)),
            pl.BlockSpec((E, 1), lambda i: (0, 0)),
        ],
        out_specs=[
            pl.BlockSpec((TOP_K, ROUTER_TILE), tok),
            pl.BlockSpec((TOP_K, ROUTER_TILE), tok),
            pl.BlockSpec((TOP_K, ROUTER_TILE), tok),
            pl.BlockSpec((E, LANES), lambda i: (0, 0)),
        ],
        out_shape=[
            jax.ShapeDtypeStruct((TOP_K, T), jnp.int32),
            jax.ShapeDtypeStruct((TOP_K, T), F32),
            jax.ShapeDtypeStruct((TOP_K, T), jnp.int32),
            jax.ShapeDtypeStruct((E, LANES), jnp.int32),
        ],
        scratch_shapes=[pltpu.VMEM((E, LANES), F32)],
        compiler_params=pltpu.CompilerParams(
            dimension_semantics=("arbitrary",), vmem_limit_bytes=VMEM_LIMIT),
        name="router",
    )(xt, router_w.T.astype(F32), router_b.reshape(E, 1).astype(F32))


def _row_tile(i):
    if isinstance(i, int):
        return pl.ds(i * SUBLANES, SUBLANES)
    return pl.ds(pl.multiple_of(i * SUBLANES, SUBLANES), SUBLANES)


def _to_row_tiles(dst_ref, val):
    n = val.shape[0]
    for s in range(SUBLANES):
        dst_ref[pl.ds(s, n, stride=SUBLANES), :] = val[:, s * LANES:(s + 1) * LANES]


def _from_row_tiles(src_ref, n):
    return jnp.concatenate(
        [src_ref[pl.ds(s, n, stride=SUBLANES), :] for s in range(SUBLANES)], axis=-1)


def _dispatch_kernel(pad_start_ref, pad_cnt_ref, nbu_ref, dest_ref, x_ref, xs_ref, src_ref, zblk_ref,
                     sems):
    TT = x_ref.shape[0]
    E = pad_start_ref.shape[0]
    RT = zblk_ref.shape[0]
    nb = xs_ref.shape[0] // RT

    _to_row_tiles(src_ref, x_ref[...])

    def issue(g, carry):
        for u in range(ISSUE_UNROLL):
            t = g * ISSUE_UNROLL + u
            for kk in range(TOP_K):
                d = dest_ref[0, 0, kk * TT + t]
                q = kk % DMA_QUEUES
                pltpu.make_async_copy(src_ref.at[_row_tile(t), :], xs_ref.at[_row_tile(d), :],
                                      sems.at[q]).start(priority=q)
        return carry

    lax.fori_loop(0, TT // ISSUE_UNROLL, issue, 0)
    for kk in range(TOP_K):
        pltpu.make_async_copy(src_ref, xs_ref.at[pl.ds(0, TT * SUBLANES), :],
                              sems.at[kk % DMA_QUEUES]).wait()

    @pl.when(pl.program_id(0) == pl.num_programs(0) - 1)
    def _():
        zblk_ref[...] = jnp.zeros_like(zblk_ref)
        sem = sems.at[0]

        def for_each_pad(fn):
            def per_expert(e, carry):
                n = pad_cnt_ref[e]
                row = pad_start_ref[e]
                p = RT // SUBLANES // 2
                while p >= 1:
                    take = (n & p) != 0
                    at = row

                    @pl.when(take)
                    def _():
                        fn(pltpu.make_async_copy(
                            zblk_ref.at[pl.ds(0, p * SUBLANES), :],
                            xs_ref.at[pl.ds(pl.multiple_of(at * SUBLANES, SUBLANES), p * SUBLANES), :],
                            sem))

                    row = row + jnp.where(take, p, 0)
                    p //= 2
                return carry
            lax.fori_loop(0, E, per_expert, 0)

            def per_block(blk, carry):
                fn(pltpu.make_async_copy(
                    zblk_ref, xs_ref.at[pl.ds(pl.multiple_of(blk * RT, RT), RT), :], sem))
                return carry
            lax.fori_loop(nbu_ref[0], nb, per_block, 0)

        for_each_pad(lambda cp: cp.start())
        for_each_pad(lambda cp: cp.wait())


def _dispatch(xt, dest_tiles, pad_start, pad_cnt, nb_used, n_rows):
    T, D = xt.shape
    nt = T // DISPATCH_TILE
    grid_spec = pltpu.PrefetchScalarGridSpec(
        num_scalar_prefetch=3,
        grid=(nt,),
        in_specs=[
            pl.BlockSpec((1, 1, TOP_K * DISPATCH_TILE), lambda i, ps, pc, nbu: (i, 0, 0),
                         memory_space=pltpu.SMEM),
            pl.BlockSpec((DISPATCH_TILE, D), lambda i, ps, pc, nbu: (i, 0)),
        ],
        out_specs=pl.BlockSpec(memory_space=pl.ANY),
        scratch_shapes=[pltpu.VMEM((DISPATCH_TILE * SUBLANES, LANES), F32),
                        pltpu.VMEM((EXPERT_ROWS * SUBLANES, LANES), F32),
                        pltpu.SemaphoreType.DMA((DMA_QUEUES,))],
    )
    assert D == SUBLANES * LANES
    return pl.pallas_call(
        _dispatch_kernel,
        grid_spec=grid_spec,
        out_shape=jax.ShapeDtypeStruct((n_rows * SUBLANES, LANES), F32),
        compiler_params=pltpu.CompilerParams(
            dimension_semantics=("arbitrary",), vmem_limit_bytes=VMEM_LIMIT),
        name="dispatch",
    )(pad_start, pad_cnt, nb_used, dest_tiles, xt)


def _cast_rows(src_ref, dst_ref):
    rows = src_ref.shape[0]

    def body(i, carry):
        r = pl.multiple_of(i * CAST_ROWS, CAST_ROWS)
        dst_ref[pl.ds(r, CAST_ROWS), :] = src_ref[pl.ds(r, CAST_ROWS), :].astype(dst_ref.dtype)
        return carry

    lax.fori_loop(0, rows // CAST_ROWS, body, 0)


def _expert_kernel(ge_ref, nbu_ref, xs_ref, wu_ref, bu_ref, wd_ref, bd_ref, ys_ref, wub_ref, wdb_ref):
    F = wd_ref.shape[0]
    b = pl.program_id(0)

    @pl.when(b >= nbu_ref[0])
    def _():
        ys_ref[...] = jnp.zeros_like(ys_ref)

    @pl.when(b < nbu_ref[0])
    def _():
        @pl.when((b == 0) | (ge_ref[b] != ge_ref[jnp.maximum(b - 1, 0)]))
        def _():
            _cast_rows(wu_ref, wub_ref)
            _cast_rows(wd_ref, wdb_ref)

        xs = _from_row_tiles(xs_ref, EXPERT_ROWS)
        hb = _dot(xs.astype(BF16), wub_ref[...]) + bu_ref[...]
        glu = jnp.minimum(hb[:, :F], SWIGLU_LIMIT)
        lin = jnp.clip(hb[:, F:], -SWIGLU_LIMIT, SWIGLU_LIMIT)
        act = glu * jax.nn.sigmoid(SWIGLU_ALPHA * glu) * (lin + 1.0)
        _to_row_tiles(ys_ref, _dot(act.astype(BF16), wdb_ref[...]) + bd_ref[...])


def _experts(xs, group_expert, nb_used, layer, w_up, b_up, w_down, b_down):
    depth, E, D, F2 = w_up.shape
    F = w_down.shape[2]
    RT = EXPERT_ROWS * SUBLANES
    nb = xs.shape[0] // RT
    assert D % CAST_ROWS == 0 and F % CAST_ROWS == 0
    row_block = lambda b, ge, nbu: (jnp.minimum(b, nbu[0] - 1), 0)
    expert_block = lambda b, ge, nbu: (layer, ge[b], 0, 0)
    grid_spec = pltpu.PrefetchScalarGridSpec(
        num_scalar_prefetch=2,
        grid=(nb,),
        in_specs=[
            pl.BlockSpec((RT, LANES), row_block),
            pl.BlockSpec((None, None, D, F2), expert_block),
            pl.BlockSpec((None, None, 1, F2), expert_block),
            pl.BlockSpec((None, None, F, D), expert_block),
            pl.BlockSpec((None, None, 1, D), expert_block),
        ],
        out_specs=pl.BlockSpec((RT, LANES), lambda b, ge, nbu: (b, 0)),
        scratch_shapes=[pltpu.VMEM((D, F2), BF16), pltpu.VMEM((F, D), BF16)],
    )
    return pl.pallas_call(
        _expert_kernel,
        grid_spec=grid_spec,
        out_shape=jax.ShapeDtypeStruct(xs.shape, F32),
        compiler_params=pltpu.CompilerParams(
            dimension_semantics=("arbitrary",), vmem_limit_bytes=VMEM_LIMIT),
        name="experts",
    )(group_expert, nb_used, xs, w_up, b_up.reshape(depth, E, 1, F2), w_down,
      b_down.reshape(depth, E, 1, D))


def _fetch_token_rows(idx_ref, ys_ref, buf_ref, sems, s, t, TT):
    for kk in range(TOP_K):
        d = idx_ref[0, 0, kk * TT + t]
        q = kk % DMA_QUEUES
        pltpu.make_async_copy(ys_ref.at[_row_tile(d), :], buf_ref.at[s, kk, _row_tile(t), :],
                              sems.at[s, q]).start(priority=q)


def _fetch_tile_rows(idx_ref, ys_ref, buf_ref, sems, s, TT):
    def body(g, carry):
        for u in range(ISSUE_UNROLL):
            _fetch_token_rows(idx_ref, ys_ref, buf_ref, sems, s, g * ISSUE_UNROLL + u, TT)
        return carry
    lax.fori_loop(0, TT // ISSUE_UNROLL, body, 0)


def _wait_tile_rows(ys_ref, buf_ref, sems, s, TT):
    for kk in range(TOP_K):
        pltpu.make_async_copy(ys_ref.at[pl.ds(0, TT * SUBLANES), :], buf_ref.at[s, kk],
                              sems.at[s, kk % DMA_QUEUES]).wait()


def _gated_sum(gate_ref, buf_ref, s):
    TT = gate_ref.shape[0]
    f = gate_ref[:, 0:1] * _from_row_tiles(buf_ref.at[s, 0], TT)
    for kk in range(1, TOP_K):
        f = f + gate_ref[:, kk:kk + 1] * _from_row_tiles(buf_ref.at[s, kk], TT)
    return f


def _moe_output(step, n_steps, dest_ref, dnext_ref, gate_ref, ys_ref, buf_ref, sems):
    TT = gate_ref.shape[0]
    slot = lax.rem(step, 2)

    @pl.when(step == 0)
    def _():
        _fetch_tile_rows(dest_ref, ys_ref, buf_ref, sems, 0, TT)

    @pl.when(step + 1 < n_steps)
    def _():
        _fetch_tile_rows(dnext_ref, ys_ref, buf_ref, sems, 1 - slot, TT)

    _wait_tile_rows(ys_ref, buf_ref, sems, slot, TT)
    return _gated_sum(gate_ref, buf_ref, slot)


def _moe_gather_specs(tile, n_tiles, step_of):
    slots = (1, 1, TOP_K * tile)
    in_specs = [
        pl.BlockSpec(slots, lambda *g: (step_of(*g), 0, 0), memory_space=pltpu.SMEM),
        pl.BlockSpec(slots, lambda *g: (jnp.minimum(step_of(*g) + 1, n_tiles - 1), 0, 0),
                     memory_space=pltpu.SMEM),
        pl.BlockSpec((tile, TOP_K), lambda *g: (step_of(*g), 0)),
        pl.BlockSpec(memory_space=pl.ANY),
    ]
    scratch = [pltpu.VMEM((2, TOP_K, tile * SUBLANES, LANES), F32),
               pltpu.SemaphoreType.DMA((2, DMA_QUEUES))]
    return in_specs, scratch


def _combine_kernel(dest_ref, dnext_ref, gate_ref, ys_ref, x_ref, lng_ref, lnb_ref, out_ref,
                    buf_ref, sems, *, alpha):
    f = _moe_output(pl.program_id(0), pl.num_programs(0), dest_ref, dnext_ref, gate_ref, ys_ref,
                    buf_ref, sems)
    out_ref[...] = _layer_norm(alpha * x_ref[...] + f, lng_ref[...], lnb_ref[...])


def _combine(xt, moe_out, ln_g, ln_b, alpha):
    ys, dest, gates_col = moe_out
    T, D = xt.shape
    nt = T // COMBINE_TILE
    dest_tiles = _tile_slots(dest, COMBINE_TILE)
    gather_specs, gather_scratch = _moe_gather_specs(COMBINE_TILE, nt, lambda i: i)
    kern = functools.partial(_combine_kernel, alpha=alpha)
    return pl.pallas_call(
        kern,
        grid=(nt,),
        in_specs=gather_specs + [
            pl.BlockSpec((COMBINE_TILE, D), lambda i: (i, 0)),
            pl.BlockSpec((1, D), lambda i: (0, 0)),
            pl.BlockSpec((1, D), lambda i: (0, 0)),
        ],
        out_specs=pl.BlockSpec((COMBINE_TILE, D), lambda i: (i, 0)),
        out_shape=jax.ShapeDtypeStruct((T, D), F32),
        scratch_shapes=gather_scratch,
        compiler_params=pltpu.CompilerParams(
            dimension_semantics=("arbitrary",), vmem_limit_bytes=VMEM_LIMIT),
        name="combine",
    )(dest_tiles, dest_tiles, gates_col, ys, xt, ln_g.reshape(1, D), ln_b.reshape(1, D))


def _tile_slots(dest, tile):
    K, T = dest.shape
    return dest.reshape(K, T // tile, tile).transpose(1, 0, 2).reshape(T // tile, 1, K * tile)


def _moe_experts(xt, layer, router_w, router_b, w_up, b_up, w_down, b_down):
    T, D = xt.shape
    E = router_w.shape[1]
    R = EXPERT_ROWS
    idx, gates, rank, cnt = _router(xt, router_w, router_b)
    counts = cnt[:, 0]
    padded = ((counts + R - 1) // R) * R
    pends = jnp.cumsum(padded)
    pstarts = pends - padded
    e_ids = jnp.arange(E, dtype=jnp.int32)[:, None, None]
    dest = rank + jnp.sum(jnp.where(idx[None] == e_ids, pstarts[:, None, None], 0), axis=0)
    n_rows = -(-(T * TOP_K + E * (R - 1)) // R) * R
    nb = n_rows // R
    nb_used = (pends[E - 1:] // R).astype(jnp.int32)
    blk = jnp.minimum(jnp.arange(nb, dtype=jnp.int32), nb_used - 1) * R
    group_expert = jnp.sum((pends[None, :] <= blk[:, None]).astype(jnp.int32), axis=1)
    xs = _dispatch(xt, _tile_slots(dest, DISPATCH_TILE), (pstarts + counts).astype(jnp.int32),
                   (padded - counts).astype(jnp.int32), nb_used, n_rows)
    ys = _experts(xs, group_expert, nb_used, layer, w_up, b_up, w_down, b_down)
    return ys, dest, gates.T


def kernel(x, ab_w_in, ab_b_igate, ab_b_fgate, ab_conv_w, ab_head_gain, ab_w_out, cf_w_pw1, cf_w_dw,
           cf_b_dw, cf_ln_g, cf_ln_b, cf_w_pw2, router_w, router_b, exp_w_up, exp_b_up, exp_w_down,
           exp_b_down, post_ln_g, post_ln_b):
    Bn, S, D = x.shape
    depth = post_ln_g.shape[0]
    alpha = float((2 * depth) ** 0.25)
    moe_out = None
    for layer in range(depth):
        j = layer // 2
        if layer % 2 == 0:
            if moe_out is not None:
                x = _combine(x.reshape(Bn * S, D), moe_out, post_ln_g[layer - 1, 1],
                             post_ln_b[layer - 1, 1], alpha).reshape(Bn, S, D)
            x = _mixer_ab(x, ab_w_in[j], ab_b_igate[j], ab_b_fgate[j], ab_conv_w[j], ab_head_gain[j],
                          ab_w_out[j], post_ln_g[layer, 0], post_ln_b[layer, 0], alpha)
        else:
            x = _conformer(x, moe_out, post_ln_g[layer - 1, 1], post_ln_b[layer - 1, 1], cf_w_pw1[j],
                           cf_w_dw[j], cf_b_dw[j], cf_ln_g[j], cf_ln_b[j], cf_w_pw2[j],
                           post_ln_g[layer, 0], post_ln_b[layer, 0], alpha)
        moe_out = _moe_experts(x.reshape(Bn * S, D), layer, router_w[layer], router_b[layer],
                               exp_w_up, exp_b_up, exp_w_down, exp_b_down)
    x = _combine(x.reshape(Bn * S, D), moe_out, post_ln_g[depth - 1, 1], post_ln_b[depth - 1, 1],
                 alpha)
    return x.reshape(Bn, S, D)
```

```python
import functools

import jax
import jax.numpy as jnp
from jax import lax
from jax.experimental import pallas as pl
from jax.experimental.pallas import tpu as pltpu

F32 = jnp.float32
BF16 = jnp.bfloat16

LANES = 128
SUBLANES = 8

MLSTM_HEADS = 4
MLSTM_QK_DIM = 64
MLSTM_V_DIM = 128
TOP_K = 4
SWIGLU_LIMIT = 7.0
SWIGLU_ALPHA = 1.702
LN_EPS = 1e-5
HEAD_NORM_EPS = 1e-6

SEQ_TILE = 512
MLSTM_CHUNK = 256
ROUTER_TILE = 512
DISPATCH_TILE = 512
COMBINE_TILE = 512
EXPERT_ROWS = 512
CONV_ROWS = 64
CAST_ROWS = 128
DMA_QUEUES = 2
ISSUE_UNROLL = 8
VMEM_LIMIT = 56 * 1024 * 1024


def _layer_norm(y, g, b):
    mu = jnp.mean(y, axis=-1, keepdims=True)
    yc = y - mu
    var = jnp.mean(yc * yc, axis=-1, keepdims=True)
    return yc * lax.rsqrt(var + LN_EPS) * g + b


def _log_sigmoid(x):
    return jnp.minimum(x, 0.0) - jnp.log1p(jnp.exp(-jnp.abs(x)))


def _dot(a, b):
    return jnp.dot(a, b, preferred_element_type=F32)


def _dot_nt(a, b):
    return lax.dot_general(a, b, (((1,), (1,)), ((), ())), preferred_element_type=F32)


def _dot_tn(a, b):
    return lax.dot_general(a, b, (((0,), (0,)), ((), ())), preferred_element_type=F32)


def _dot_f32(a, b):
    return jnp.dot(a, b, preferred_element_type=F32, precision=lax.Precision.HIGHEST)


def _mixer_ab_kernel(x_ref, wall_ref, wift_ref, bcol_ref, brow_ref, convw_ref, gain_ref, wout_ref,
                     lng_ref, lnb_ref, out_ref,
                     z_ref, hy_ref, ubuf_ref, c_ref, n_ref, m_ref, *, alpha, sconv_w):
    H, L, TS = MLSTM_HEADS, MLSTM_CHUNK, SEQ_TILE
    HW = H * LANES
    q0, k0, v0, o0 = 0, HW, 2 * HW, 3 * HW
    sb0 = 4 * HW
    sc0 = sb0 + sconv_w
    sx0 = sc0 + sconv_w
    if0 = sx0 + sconv_w

    @pl.when(pl.program_id(1) == 0)
    def _():
        c_ref[...] = jnp.zeros_like(c_ref)
        n_ref[...] = jnp.zeros_like(n_ref)
        m_ref[...] = jnp.zeros_like(m_ref)
        ubuf_ref[0:SUBLANES, :] = jnp.zeros((SUBLANES, sconv_w), F32)

    x = x_ref[...]
    xb = x.astype(BF16)
    z_ref[...] = _dot(xb, wall_ref[...])
    pre_row = _dot_nt(wift_ref[...], xb) + brow_ref[...]
    logf_row = _log_sigmoid(pre_row)
    pre_col = z_ref[:, if0:if0 + LANES] + bcol_ref[...]
    logf_col = _log_sigmoid(pre_col)

    r_i = lax.broadcasted_iota(jnp.int32, (L, L), 0)
    c_i = lax.broadcasted_iota(jnp.int32, (L, L), 1)
    causal = r_i >= c_i
    tril = causal.astype(F32)
    triu = (r_i <= c_i).astype(F32)

    for c in range(TS // L):
        r0 = c * L
        b_col = _dot_f32(tril, logf_col[r0:r0 + L, :])
        b_row = _dot_f32(logf_row[:, r0:r0 + L], triu)
        for h in range(H):
            q = z_ref[r0:r0 + L, q0 + h * LANES:q0 + (h + 1) * LANES]
            k = z_ref[r0:r0 + L, k0 + h * LANES:k0 + (h + 1) * LANES]
            v = z_ref[r0:r0 + L, v0 + h * LANES:v0 + (h + 1) * LANES]
            o = z_ref[r0:r0 + L, o0 + h * LANES:o0 + (h + 1) * LANES]
            qb, kb, vb = q.astype(BF16), k.astype(BF16), v.astype(BF16)
            bc = b_col[:, H + h:H + h + 1]
            ic = pre_col[r0:r0 + L, h:h + 1]
            br = b_row[H + h:H + h + 1, :]
            ir = pre_row[h:h + 1, r0:r0 + L]
            c_prev = c_ref[h]
            n_prev = n_ref[h]
            m_prev = m_ref[h][:, 0:1]

            dmat = jnp.where(causal, bc - br + ir, -jnp.inf)
            e_inter = bc + m_prev
            m_t = jnp.maximum(e_inter, jnp.max(dmat, axis=-1, keepdims=True))
            w_intra = jnp.exp(dmat - m_t)
            w_inter = jnp.exp(e_inter - m_t)
            s = _dot_nt(qb, kb) * w_intra
            num = w_inter * _dot(qb, c_prev.astype(BF16)) + _dot(s.astype(BF16), vb)
            den = (w_inter * jnp.sum(q * n_prev, axis=-1, keepdims=True)
                   + jnp.sum(s, axis=-1, keepdims=True))
            hh = num / jnp.maximum(jnp.abs(den), jnp.exp(-m_t))

            g = bc[L - 1:L, :]
            a_col = g - bc + ic
            m_new = jnp.maximum(g + m_prev, jnp.max(a_col, axis=0, keepdims=True))
            decay = jnp.exp(g + m_prev - m_new)
            kw = k * jnp.exp(a_col - m_new)
            c_ref[h] = decay * c_prev + _dot_tn(kw.astype(BF16), vb)
            n_ref[h] = decay * n_prev + jnp.sum(kw, axis=0, keepdims=True)
            m_ref[h] = jnp.broadcast_to(m_new, (1, LANES))

            mu = jnp.mean(hh, axis=-1, keepdims=True)
            hc = hh - mu
            var = jnp.mean(hc * hc, axis=-1, keepdims=True)
            hn = hc * lax.rsqrt(var + HEAD_NORM_EPS) * gain_ref[:, h * LANES:(h + 1) * LANES]
            hy_ref[r0:r0 + L, h * LANES:(h + 1) * LANES] = (jax.nn.sigmoid(o) * hn).astype(BF16)

    u = z_ref[:, sc0:sc0 + sconv_w] * z_ref[:, sx0:sx0 + sconv_w]
    ubuf_ref[SUBLANES:SUBLANES + TS, :] = u
    conv = (convw_ref[2:3, :] * u
            + convw_ref[1:2, :] * ubuf_ref[SUBLANES - 1:SUBLANES - 1 + TS, :]
            + convw_ref[0:1, :] * ubuf_ref[SUBLANES - 2:SUBLANES - 2 + TS, :])
    hy_ref[:, HW:HW + sconv_w] = (z_ref[:, sb0:sb0 + sconv_w] * conv).astype(BF16)
    ubuf_ref[0:SUBLANES, :] = ubuf_ref[TS:TS + SUBLANES, :]

    mix = _dot(hy_ref[...], wout_ref[...])
    out_ref[...] = _layer_norm(alpha * x + mix, lng_ref[...], lnb_ref[...])


def _mixer_ab(x, w_in, b_igate, b_fgate, conv_w, head_gain, w_out, ln_g, ln_b, alpha):
    Bn, S, D = x.shape
    H, dk, dv = MLSTM_HEADS, MLSTM_QK_DIM, MLSTM_V_DIM
    assert dv == LANES and dk <= LANES and S % SEQ_TILE == 0 and SEQ_TILE % MLSTM_CHUNK == 0
    hw = H * dv
    sw = D - hw
    cuts = [H * dk, 2 * H * dk, 2 * H * dk + hw, 2 * H * dk + 2 * hw]
    wq, wk, wv, wo = (w_in[:, a:b] for a, b in zip([0] + cuts[:-1], cuts))
    wi = w_in[:, cuts[-1]:cuts[-1] + H]
    wf = w_in[:, cuts[-1] + H:cuts[-1] + 2 * H]
    ws = w_in[:, cuts[-1] + 2 * H:]

    def pad_heads(w, scale):
        w = (w * scale).reshape(D, H, dk)
        return jnp.pad(w, ((0, 0), (0, 0), (0, LANES - dk))).reshape(D, H * LANES)

    w_if = jnp.concatenate([wi, wf], axis=1)
    w_all = jnp.concatenate(
        [pad_heads(wq, 1.0), pad_heads(wk, dk ** -0.5), wv, wo, ws,
         jnp.pad(w_if, ((0, 0), (0, LANES - 2 * H)))], axis=1).astype(BF16)
    nz = w_all.shape[1]
    w_if_t = w_if.T.astype(BF16)
    b_if = jnp.concatenate([b_igate, b_fgate]).astype(F32)
    b_col = jnp.pad(b_if, (0, LANES - 2 * H)).reshape(1, LANES)
    b_row = b_if.reshape(2 * H, 1)

    const = lambda b, j: (0, 0)
    kern = functools.partial(_mixer_ab_kernel, alpha=alpha, sconv_w=sw)
    return pl.pallas_call(
        kern,
        grid=(Bn, S // SEQ_TILE),
        in_specs=[
            pl.BlockSpec((None, SEQ_TILE, D), lambda b, j: (b, j, 0)),
            pl.BlockSpec((D, nz), const),
            pl.BlockSpec((2 * H, D), const),
            pl.BlockSpec((1, LANES), const),
            pl.BlockSpec((2 * H, 1), const),
            pl.BlockSpec((conv_w.shape[0], sw), const),
            pl.BlockSpec((1, hw), const),
            pl.BlockSpec((D, D), const),
            pl.BlockSpec((1, D), const),
            pl.BlockSpec((1, D), const),
        ],
        out_specs=pl.BlockSpec((None, SEQ_TILE, D), lambda b, j: (b, j, 0)),
        out_shape=jax.ShapeDtypeStruct((Bn, S, D), F32),
        scratch_shapes=[
            pltpu.VMEM((SEQ_TILE, nz), F32),
            pltpu.VMEM((SEQ_TILE, D), BF16),
            pltpu.VMEM((SEQ_TILE + 2 * SUBLANES, sw), F32),
            pltpu.VMEM((H, LANES, dv), F32),
            pltpu.VMEM((H, 1, LANES), F32),
            pltpu.VMEM((H, 1, LANES), F32),
        ],
        compiler_params=pltpu.CompilerParams(
            dimension_semantics=("parallel", "arbitrary"), vmem_limit_bytes=VMEM_LIMIT),
        name="mixer_ab",
    )(x, w_all, w_if_t, b_col, b_row, conv_w.astype(F32), head_gain.reshape(1, hw).astype(F32),
      w_out.astype(BF16), ln_g.reshape(1, D), ln_b.reshape(1, D))


def _conformer_kernel(dest_ref, dnext_ref, gate_ref, ys_ref, x_ref, mlng_ref, mlnb_ref,
                      w1_ref, wdw_ref, bdw_ref, cg_ref, cb_ref, w2_ref, lng_ref, lnb_ref,
                      out_ref, ubuf_ref, uc_ref, sh_ref, buf_ref, sems, *, alpha, taps, hist):
    TS = SEQ_TILE
    C = uc_ref.shape[1]

    @pl.when(pl.program_id(1) == 0)
    def _():
        ubuf_ref[0:hist, :] = jnp.zeros((hist, C), F32)

    step = pl.program_id(0) * pl.num_programs(1) + pl.program_id(1)
    n_steps = pl.num_programs(0) * pl.num_programs(1)
    slot = lax.rem(step, 2)

    @pl.when(step == 0)
    def _():
        _fetch_tile_rows(dest_ref, ys_ref, buf_ref, sems, 0, TS)

    _wait_tile_rows(ys_ref, buf_ref, sems, slot, TS)
    f = _gated_sum(gate_ref, buf_ref, slot)
    x = _layer_norm(alpha * x_ref[...] + f, mlng_ref[...], mlnb_ref[...])
    z = _dot(x.astype(BF16), w1_ref[...])
    ubuf_ref[hist:hist + TS, :] = z[:, :C] * jax.nn.sigmoid(z[:, C:])

    base = hist - (taps - 1)
    sh_rows = sh_ref.shape[1]
    n_conv_blocks = (C // LANES) * (TS // CONV_ROWS)
    fetch_per_block = TS // n_conv_blocks
    assert fetch_per_block * n_conv_blocks == TS
    for cb in range(C // LANES):
        cs = slice(cb * LANES, (cb + 1) * LANES)
        for r in range(1, SUBLANES):
            sh_ref[r - 1] = ubuf_ref[r:r + sh_rows, cs]
        for rb in range(TS // CONV_ROWS):
            t0 = (cb * (TS // CONV_ROWS) + rb) * fetch_per_block
            for t in range(t0, t0 + fetch_per_block):
                _fetch_token_rows(dnext_ref, ys_ref, buf_ref, sems, 1 - slot, t, TS)
            acc = jnp.zeros((CONV_ROWS, LANES), F32) + bdw_ref[:, cs]
            for j in range(taps):
                r = (base + j) % SUBLANES
                start = rb * CONV_ROWS + base + j - r
                if r == 0:
                    win = ubuf_ref[start:start + CONV_ROWS, cs]
                else:
                    win = sh_ref[r - 1, start:start + CONV_ROWS, :]
                acc = acc + wdw_ref[j:j + 1, cs] * win
            uc_ref[rb * CONV_ROWS:(rb + 1) * CONV_ROWS, cs] = acc
    ubuf_ref[0:hist, :] = ubuf_ref[TS:TS + hist, :]

    un = _layer_norm(uc_ref[...], cg_ref[...], cb_ref[...])
    act = un * jax.nn.sigmoid(un)
    mix = _dot(act.astype(BF16), w2_ref[...])
    out_ref[...] = _layer_norm(alpha * x + mix, lng_ref[...], lnb_ref[...])

    @pl.when(step == n_steps - 1)
    def _():
        _wait_tile_rows(ys_ref, buf_ref, sems, 1 - slot, TS)


def _conformer(x, moe_out, moe_ln_g, moe_ln_b, w_pw1, w_dw, b_dw, cg, cb, w_pw2, ln_g, ln_b, alpha):
    ys, dest, gates_col = moe_out
    Bn, S, D = x.shape
    taps, C = w_dw.shape
    hist = -(-(taps - 1) // SUBLANES) * SUBLANES
    assert S % SEQ_TILE == 0 and C % LANES == 0 and hist <= SEQ_TILE
    nj = S // SEQ_TILE
    const = lambda b, j: (0, 0)
    dest_tiles = _tile_slots(dest, SEQ_TILE)
    gather_specs, gather_scratch = _moe_gather_specs(SEQ_TILE, Bn * nj, lambda b, j: b * nj + j)
    kern = functools.partial(_conformer_kernel, alpha=alpha, taps=taps, hist=hist)
    return pl.pallas_call(
        kern,
        grid=(Bn, nj),
        in_specs=gather_specs + [
            pl.BlockSpec((None, SEQ_TILE, D), lambda b, j: (b, j, 0)),
            pl.BlockSpec((1, D), const),
            pl.BlockSpec((1, D), const),
            pl.BlockSpec((D, 2 * C), const),
            pl.BlockSpec((taps, C), const),
            pl.BlockSpec((1, C), const),
            pl.BlockSpec((1, C), const),
            pl.BlockSpec((1, C), const),
            pl.BlockSpec((C, D), const),
            pl.BlockSpec((1, D), const),
            pl.BlockSpec((1, D), const),
        ],
        out_specs=pl.BlockSpec((None, SEQ_TILE, D), lambda b, j: (b, j, 0)),
        out_shape=jax.ShapeDtypeStruct((Bn, S, D), F32),
        scratch_shapes=[
            pltpu.VMEM((SEQ_TILE + hist, C), F32),
            pltpu.VMEM((SEQ_TILE, C), F32),
            pltpu.VMEM((SUBLANES - 1, SEQ_TILE + hist - SUBLANES, LANES), F32),
        ] + gather_scratch,
        compiler_params=pltpu.CompilerParams(
            dimension_semantics=("arbitrary", "arbitrary"), vmem_limit_bytes=VMEM_LIMIT),
        name="conformer",
    )(dest_tiles, dest_tiles, gates_col, ys, x, moe_ln_g.reshape(1, D), moe_ln_b.reshape(1, D),
      w_pw1.astype(BF16), w_dw.astype(F32), b_dw.reshape(1, C), cg.reshape(1, C),
      cb.reshape(1, C), w_pw2.astype(BF16), ln_g.reshape(1, D), ln_b.reshape(1, D))


def _router_kernel(x_ref, rwt_ref, rb_ref, idx_ref, gate_ref, rank_ref, cnt_ref, base_ref):
    E = rwt_ref.shape[0]
    TT = x_ref.shape[0]

    @pl.when(pl.program_id(0) == 0)
    def _():
        base_ref[...] = jnp.zeros_like(base_ref)

    logits = lax.dot_general(rwt_ref[...], x_ref[...], (((1,), (1,)), ((), ())),
                             preferred_element_type=F32,
                             precision=lax.Precision.HIGHEST) + rb_ref[...]
    e_iota = lax.broadcasted_iota(jnp.int32, (E, TT), 0)
    work = logits
    vals, idxs, hots = [], [], []
    for _ in range(TOP_K):
        mx = jnp.max(work, axis=0, keepdims=True)
        ix = jnp.min(jnp.where(work == mx, e_iota, E), axis=0, keepdims=True)
        hot = e_iota == ix
        work = jnp.where(hot, -jnp.inf, work)
        vals.append(mx)
        idxs.append(ix)
        hots.append(hot)
    ex = [jnp.exp(v - vals[0]) for v in vals]
    tot = functools.reduce(lambda a, b: a + b, ex)

    member = functools.reduce(lambda a, b: a | b, hots).astype(F32)
    s_i = lax.broadcasted_iota(jnp.int32, (TT, TT), 0)
    t_i = lax.broadcasted_iota(jnp.int32, (TT, TT), 1)
    before = (s_i < t_i).astype(BF16)
    pos = _dot(member.astype(BF16), before) + base_ref[:, 0:1]
    for kk in range(TOP_K):
        idx_ref[kk:kk + 1, :] = idxs[kk]
        gate_ref[kk:kk + 1, :] = ex[kk] / tot
        rank_ref[kk:kk + 1, :] = jnp.sum(jnp.where(hots[kk], pos, 0.0), axis=0,
                                         keepdims=True).astype(jnp.int32)
    base_ref[...] = base_ref[...] + jnp.sum(member, axis=1, keepdims=True)
    cnt_ref[...] = base_ref[...].astype(jnp.int32)


def _router(xt, router_w, router_b):
    T, D = xt.shape
    E = router_w.shape[1]
    assert T % ROUTER_TILE == 0
    tok = lambda i: (0, i)
    return pl.pallas_call(
        _router_kernel,
        grid=(T // ROUTER_TILE,),
        in_specs=[
            pl.BlockSpec((ROUTER_TILE, D), lambda i: (i, 0)),
            pl.BlockSpec((E, D), lambda i: (0, 0)),
            pl.BlockSpec((E, 1), lambda i: (0, 0)),
        ],
        out_specs=[
            pl.BlockSpec((TOP_K, ROUTER_TILE), tok),
            pl.BlockSpec((TOP_K, ROUTER_TILE), tok),
            pl.BlockSpec((TOP_K, ROUTER_TILE), tok),
            pl.BlockSpec((E, LANES), lambda i: (0, 0)),
        ],
        out_shape=[
            jax.ShapeDtypeStruct((TOP_K, T), jnp.int32),
            jax.ShapeDtypeStruct((TOP_K, T), F32),
            jax.ShapeDtypeStruct((TOP_K, T), jnp.int32),
            jax.ShapeDtypeStruct((E, LANES), jnp.int32),
        ],
        scratch_shapes=[pltpu.VMEM((E, LANES), F32)],
        compiler_params=pltpu.CompilerParams(
            dimension_semantics=("arbitrary",), vmem_limit_bytes=VMEM_LIMIT),
        name="router",
    )(xt, router_w.T.astype(F32), router_b.reshape(E, 1).astype(F32))


def _row_tile(i):
    if isinstance(i, int):
        return pl.ds(i * SUBLANES, SUBLANES)
    return pl.ds(pl.multiple_of(i * SUBLANES, SUBLANES), SUBLANES)


def _to_row_tiles(dst_ref, val):
    n = val.shape[0]
    for s in range(SUBLANES):
        dst_ref[pl.ds(s, n, stride=SUBLANES), :] = val[:, s * LANES:(s + 1) * LANES]


def _from_row_tiles(src_ref, n):
    return jnp.concatenate(
        [src_ref[pl.ds(s, n, stride=SUBLANES), :] for s in range(SUBLANES)], axis=-1)


def _dispatch_kernel(pad_start_ref, pad_cnt_ref, nbu_ref, dest_ref, x_ref, xs_ref, src_ref, zblk_ref,
                     sems):
    TT = x_ref.shape[0]
    E = pad_start_ref.shape[0]
    RT = zblk_ref.shape[0]
    nb = xs_ref.shape[0] // RT

    _to_row_tiles(src_ref, x_ref[...])

    def issue(g, carry):
        for u in range(ISSUE_UNROLL):
            t = g * ISSUE_UNROLL + u
            for kk in range(TOP_K):
                d = dest_ref[0, 0, kk * TT + t]
                q = kk % DMA_QUEUES
                pltpu.make_async_copy(src_ref.at[_row_tile(t), :], xs_ref.at[_row_tile(d), :],
                                      sems.at[q]).start(priority=q)
        return carry

    lax.fori_loop(0, TT // ISSUE_UNROLL, issue, 0)
    for kk in range(TOP_K):
        pltpu.make_async_copy(src_ref, xs_ref.at[pl.ds(0, TT * SUBLANES), :],
                              sems.at[kk % DMA_QUEUES]).wait()

    @pl.when(pl.program_id(0) == pl.num_programs(0) - 1)
    def _():
        zblk_ref[...] = jnp.zeros_like(zblk_ref)
        sem = sems.at[0]

        def for_each_pad(fn):
            def per_expert(e, carry):
                n = pad_cnt_ref[e]
                row = pad_start_ref[e]
                p = RT // SUBLANES // 2
                while p >= 1:
                    take = (n & p) != 0
                    at = row

                    @pl.when(take)
                    def _():
                        fn(pltpu.make_async_copy(
                            zblk_ref.at[pl.ds(0, p * SUBLANES), :],
                            xs_ref.at[pl.ds(pl.multiple_of(at * SUBLANES, SUBLANES), p * SUBLANES), :],
                            sem))

                    row = row + jnp.where(take, p, 0)
                    p //= 2
                return carry
            lax.fori_loop(0, E, per_expert, 0)

            def per_block(blk, carry):
                fn(pltpu.make_async_copy(
                    zblk_ref, xs_ref.at[pl.ds(pl.multiple_of(blk * RT, RT), RT), :], sem))
                return carry
            lax.fori_loop(nbu_ref[0], nb, per_block, 0)

        for_each_pad(lambda cp: cp.start())
        for_each_pad(lambda cp: cp.wait())


def _dispatch(xt, dest_tiles, pad_start, pad_cnt, nb_used, n_rows):
    T, D = xt.shape
    nt = T // DISPATCH_TILE
    grid_spec = pltpu.PrefetchScalarGridSpec(
        num_scalar_prefetch=3,
        grid=(nt,),
        in_specs=[
            pl.BlockSpec((1, 1, TOP_K * DISPATCH_TILE), lambda i, ps, pc, nbu: (i, 0, 0),
                         memory_space=pltpu.SMEM),
            pl.BlockSpec((DISPATCH_TILE, D), lambda i, ps, pc, nbu: (i, 0)),
        ],
        out_specs=pl.BlockSpec(memory_space=pl.ANY),
        scratch_shapes=[pltpu.VMEM((DISPATCH_TILE * SUBLANES, LANES), F32),
                        pltpu.VMEM((EXPERT_ROWS * SUBLANES, LANES), F32),
                        pltpu.SemaphoreType.DMA((DMA_QUEUES,))],
    )
    assert D == SUBLANES * LANES
    return pl.pallas_call(
        _dispatch_kernel,
        grid_spec=grid_spec,
        out_shape=jax.ShapeDtypeStruct((n_rows * SUBLANES, LANES), F32),
        compiler_params=pltpu.CompilerParams(
            dimension_semantics=("arbitrary",), vmem_limit_bytes=VMEM_LIMIT),
        name="dispatch",
    )(pad_start, pad_cnt, nb_used, dest_tiles, xt)


def _cast_rows(src_ref, dst_ref):
    rows = src_ref.shape[0]

    def body(i, carry):
        r = pl.multiple_of(i * CAST_ROWS, CAST_ROWS)
        dst_ref[pl.ds(r, CAST_ROWS), :] = src_ref[pl.ds(r, CAST_ROWS), :].astype(dst_ref.dtype)
        return carry

    lax.fori_loop(0, rows // CAST_ROWS, body, 0)


def _expert_kernel(ge_ref, nbu_ref, xs_ref, wu_ref, bu_ref, wd_ref, bd_ref, ys_ref, wub_ref, wdb_ref):
    F = wd_ref.shape[0]
    b = pl.program_id(0)

    @pl.when(b >= nbu_ref[0])
    def _():
        ys_ref[...] = jnp.zeros_like(ys_ref)

    @pl.when(b < nbu_ref[0])
    def _():
        @pl.when((b == 0) | (ge_ref[b] != ge_ref[jnp.maximum(b - 1, 0)]))
        def _():
            _cast_rows(wu_ref, wub_ref)
            _cast_rows(wd_ref, wdb_ref)

        xs = _from_row_tiles(xs_ref, EXPERT_ROWS)
        hb = _dot(xs.astype(BF16), wub_ref[...]) + bu_ref[...]
        glu = jnp.minimum(hb[:, :F], SWIGLU_LIMIT)
        lin = jnp.clip(hb[:, F:], -SWIGLU_LIMIT, SWIGLU_LIMIT)
        act = glu * jax.nn.sigmoid(SWIGLU_ALPHA * glu) * (lin + 1.0)
        _to_row_tiles(ys_ref, _dot(act.astype(BF16), wdb_ref[...]) + bd_ref[...])


def _experts(xs, group_expert, nb_used, layer, w_up, b_up, w_down, b_down):
    depth, E, D, F2 = w_up.shape
    F = w_down.shape[2]
    RT = EXPERT_ROWS * SUBLANES
    nb = xs.shape[0] // RT
    assert D % CAST_ROWS == 0 and F % CAST_ROWS == 0
    row_block = lambda b, ge, nbu: (jnp.minimum(b, nbu[0] - 1), 0)
    expert_block = lambda b, ge, nbu: (layer, ge[b], 0, 0)
    grid_spec = pltpu.PrefetchScalarGridSpec(
        num_scalar_prefetch=2,
        grid=(nb,),
        in_specs=[
            pl.BlockSpec((RT, LANES), row_block),
            pl.BlockSpec((None, None, D, F2), expert_block),
            pl.BlockSpec((None, None, 1, F2), expert_block),
            pl.BlockSpec((None, None, F, D), expert_block),
            pl.BlockSpec((None, None, 1, D), expert_block),
        ],
        out_specs=pl.BlockSpec((RT, LANES), lambda b, ge, nbu: (b, 0)),
        scratch_shapes=[pltpu.VMEM((D, F2), BF16), pltpu.VMEM((F, D), BF16)],
    )
    return pl.pallas_call(
        _expert_kernel,
        grid_spec=grid_spec,
        out_shape=jax.ShapeDtypeStruct(xs.shape, F32),
        compiler_params=pltpu.CompilerParams(
            dimension_semantics=("arbitrary",), vmem_limit_bytes=VMEM_LIMIT),
        name="experts",
    )(group_expert, nb_used, xs, w_up, b_up.reshape(depth, E, 1, F2), w_down,
      b_down.reshape(depth, E, 1, D))


def _fetch_token_rows(idx_ref, ys_ref, buf_ref, sems, s, t, TT):
    for kk in range(TOP_K):
        d = idx_ref[0, 0, kk * TT + t]
        q = kk % DMA_QUEUES
        pltpu.make_async_copy(ys_ref.at[_row_tile(d), :], buf_ref.at[s, kk, _row_tile(t), :],
                              sems.at[s, q]).start(priority=q)


def _fetch_tile_rows(idx_ref, ys_ref, buf_ref, sems, s, TT):
    def body(g, carry):
        for u in range(ISSUE_UNROLL):
            _fetch_token_rows(idx_ref, ys_ref, buf_ref, sems, s, g * ISSUE_UNROLL + u, TT)
        return carry
    lax.fori_loop(0, TT // ISSUE_UNROLL, body, 0)


def _wait_tile_rows(ys_ref, buf_ref, sems, s, TT):
    for kk in range(TOP_K):
        pltpu.make_async_copy(ys_ref.at[pl.ds(0, TT * SUBLANES), :], buf_ref.at[s, kk],
                              sems.at[s, kk % DMA_QUEUES]).wait()


def _gated_sum(gate_ref, buf_ref, s):
    TT = gate_ref.shape[0]
    f = gate_ref[:, 0:1] * _from_row_tiles(buf_ref.at[s, 0], TT)
    for kk in range(1, TOP_K):
        f = f + gate_ref[:, kk:kk + 1] * _from_row_tiles(buf_ref.at[s, kk], TT)
    return f


def _moe_output(step, n_steps, dest_ref, dnext_ref, gate_ref, ys_ref, buf_ref, sems):
    TT = gate_ref.shape[0]
    slot = lax.rem(step, 2)

    @pl.when(step == 0)
    def _():
        _fetch_tile_rows(dest_ref, ys_ref, buf_ref, sems, 0, TT)

    @pl.when(step + 1 < n_steps)
    def _():
        _fetch_tile_rows(dnext_ref, ys_ref, buf_ref, sems, 1 - slot, TT)

    _wait_tile_rows(ys_ref, buf_ref, sems, slot, TT)
    return _gated_sum(gate_ref, buf_ref, slot)


def _moe_gather_specs(tile, n_tiles, step_of):
    slots = (1, 1, TOP_K * tile)
    in_specs = [
        pl.BlockSpec(slots, lambda *g: (step_of(*g), 0, 0), memory_space=pltpu.SMEM),
        pl.BlockSpec(slots, lambda *g: (jnp.minimum(step_of(*g) + 1, n_tiles - 1), 0, 0),
                     memory_space=pltpu.SMEM),
        pl.BlockSpec((tile, TOP_K), lambda *g: (step_of(*g), 0)),
        pl.BlockSpec(memory_space=pl.ANY),
    ]
    scratch = [pltpu.VMEM((2, TOP_K, tile * SUBLANES, LANES), F32),
               pltpu.SemaphoreType.DMA((2, DMA_QUEUES))]
    return in_specs, scratch


def _combine_kernel(dest_ref, dnext_ref, gate_ref, ys_ref, x_ref, lng_ref, lnb_ref, out_ref,
                    buf_ref, sems, *, alpha):
    f = _moe_output(pl.program_id(0), pl.num_programs(0), dest_ref, dnext_ref, gate_ref, ys_ref,
                    buf_ref, sems)
    out_ref[...] = _layer_norm(alpha * x_ref[...] + f, lng_ref[...], lnb_ref[...])


def _combine(xt, moe_out, ln_g, ln_b, alpha):
    ys, dest, gates_col = moe_out
    T, D = xt.shape
    nt = T // COMBINE_TILE
    dest_tiles = _tile_slots(dest, COMBINE_TILE)
    gather_specs, gather_scratch = _moe_gather_specs(COMBINE_TILE, nt, lambda i: i)
    kern = functools.partial(_combine_kernel, alpha=alpha)
    return pl.pallas_call(
        kern,
        grid=(nt,),
        in_specs=gather_specs + [
            pl.BlockSpec((COMBINE_TILE, D), lambda i: (i, 0)),
            pl.BlockSpec((1, D), lambda i: (0, 0)),
            pl.BlockSpec((1, D), lambda i: (0, 0)),
        ],
        out_specs=pl.BlockSpec((COMBINE_TILE, D), lambda i: (i, 0)),
        out_shape=jax.ShapeDtypeStruct((T, D), F32),
        scratch_shapes=gather_scratch,
        compiler_params=pltpu.CompilerParams(
            dimension_semantics=("arbitrary",), vmem_limit_bytes=VMEM_LIMIT),
        name="combine",
    )(dest_tiles, dest_tiles, gates_col, ys, xt, ln_g.reshape(1, D), ln_b.reshape(1, D))


def _tile_slots(dest, tile):
    K, T = dest.shape
    return dest.reshape(K, T // tile, tile).transpose(1, 0, 2).reshape(T // tile, 1, K * tile)


def _moe_experts(xt, layer, router_w, router_b, w_up, b_up, w_down, b_down):
    T, D = xt.shape
    E = router_w.shape[1]
    R = EXPERT_ROWS
    idx, gates, rank, cnt = _router(xt, router_w, router_b)
    counts = cnt[:, 0]
    padded = ((counts + R - 1) // R) * R
    pends = jnp.cumsum(padded)
    pstarts = pends - padded
    e_ids = jnp.arange(E, dtype=jnp.int32)[:, None, None]
    dest = rank + jnp.sum(jnp.where(idx[None] == e_ids, pstarts[:, None, None], 0), axis=0)
    n_rows = -(-(T * TOP_K + E * (R - 1)) // R) * R
    nb = n_rows // R
    nb_used = (pends[E - 1:] // R).astype(jnp.int32)
    blk = jnp.minimum(jnp.arange(nb, dtype=jnp.int32), nb_used - 1) * R
    group_expert = jnp.sum((pends[None, :] <= blk[:, None]).astype(jnp.int32), axis=1)
    xs = _dispatch(xt, _tile_slots(dest, DISPATCH_TILE), (pstarts + counts).astype(jnp.int32),
                   (padded - counts).astype(jnp.int32), nb_used, n_rows)
    ys = _experts(xs, group_expert, nb_used, layer, w_up, b_up, w_down, b_down)
    return ys, dest, gates.T


def kernel(x, ab_w_in, ab_b_igate, ab_b_fgate, ab_conv_w, ab_head_gain, ab_w_out, cf_w_pw1, cf_w_dw,
           cf_b_dw, cf_ln_g, cf_ln_b, cf_w_pw2, router_w, router_b, exp_w_up, exp_b_up, exp_w_down,
           exp_b_down, post_ln_g, post_ln_b):
    Bn, S, D = x.shape
    depth = post_ln_g.shape[0]
    alpha = float((2 * depth) ** 0.25)
    moe_out = None
    for layer in range(depth):
        j = layer // 2
        if layer % 2 == 0:
            if moe_out is not None:
                x = _combine(x.reshape(Bn * S, D), moe_out, post_ln_g[layer - 1, 1],
                             post_ln_b[layer - 1, 1], alpha).reshape(Bn, S, D)
            x = _mixer_ab(x, ab_w_in[j], ab_b_igate[j], ab_b_fgate[j], ab_conv_w[j], ab_head_gain[j],
                          ab_w_out[j], post_ln_g[layer, 0], post_ln_b[layer, 0], alpha)
        else:
            x = _conformer(x, moe_out, post_ln_g[layer - 1, 1], post_ln_b[layer - 1, 1], cf_w_pw1[j],
                           cf_w_dw[j], cf_b_dw[j], cf_ln_g[j], cf_ln_b[j], cf_w_pw2[j],
                           post_ln_g[layer, 0], post_ln_b[layer, 0], alpha)
        moe_out = _moe_experts(x.reshape(Bn * S, D), layer, router_w[layer], router_b[layer],
                               exp_w_up, exp_b_up, exp_w_down, exp_b_down)
    x = _combine(x.reshape(Bn * S, D), moe_out, post_ln_g[depth - 1, 1], post_ln_b[depth - 1, 1],
                 alpha)
    return x.reshape(Bn, S, D)
```

```python
import functools

import jax
import jax.numpy as jnp
from jax import lax
from jax.experimental import pallas as pl
from jax.experimental.pallas import tpu as pltpu

F32 = jnp.float32
BF16 = jnp.bfloat16

LANES = 128
SUBLANES = 8

MLSTM_HEADS = 4
MLSTM_QK_DIM = 64
MLSTM_V_DIM = 128
TOP_K = 4
SWIGLU_LIMIT = 7.0
SWIGLU_ALPHA = 1.702
LN_EPS = 1e-5
HEAD_NORM_EPS = 1e-6

SEQ_TILE = 512
MLSTM_CHUNK = 256
ROUTER_TILE = 512
DISPATCH_TILE = 512
COMBINE_TILE = 512
EXPERT_ROWS = 512
CONV_ROWS = 64
CAST_ROWS = 128
DMA_QUEUES = 2
ISSUE_UNROLL = 16
VMEM_LIMIT = 56 * 1024 * 1024


def _layer_norm(y, g, b):
    mu = jnp.mean(y, axis=-1, keepdims=True)
    yc = y - mu
    var = jnp.mean(yc * yc, axis=-1, keepdims=True)
    return yc * lax.rsqrt(var + LN_EPS) * g + b


def _log_sigmoid(x):
    return jnp.minimum(x, 0.0) - jnp.log1p(jnp.exp(-jnp.abs(x)))


def _dot(a, b):
    return jnp.dot(a, b, preferred_element_type=F32)


def _dot_nt(a, b):
    return lax.dot_general(a, b, (((1,), (1,)), ((), ())), preferred_element_type=F32)


def _dot_tn(a, b):
    return lax.dot_general(a, b, (((0,), (0,)), ((), ())), preferred_element_type=F32)


def _dot_f32(a, b):
    return jnp.dot(a, b, preferred_element_type=F32, precision=lax.Precision.HIGHEST)


def _mixer_ab_kernel(x_ref, wall_ref, wift_ref, bcol_ref, brow_ref, convw_ref, gain_ref, wout_ref,
                     lng_ref, lnb_ref, out_ref,
                     z_ref, hy_ref, ubuf_ref, c_ref, n_ref, m_ref, *, alpha, sconv_w):
    H, L, TS = MLSTM_HEADS, MLSTM_CHUNK, SEQ_TILE
    HW = H * LANES
    q0, k0, v0, o0 = 0, HW, 2 * HW, 3 * HW
    sb0 = 4 * HW
    sc0 = sb0 + sconv_w
    sx0 = sc0 + sconv_w
    if0 = sx0 + sconv_w

    @pl.when(pl.program_id(1) == 0)
    def _():
        c_ref[...] = jnp.zeros_like(c_ref)
        n_ref[...] = jnp.zeros_like(n_ref)
        m_ref[...] = jnp.zeros_like(m_ref)
        ubuf_ref[0:SUBLANES, :] = jnp.zeros((SUBLANES, sconv_w), F32)

    x = x_ref[...]
    xb = x.astype(BF16)
    z_ref[...] = _dot(xb, wall_ref[...])
    pre_row = _dot_nt(wift_ref[...], xb) + brow_ref[...]
    logf_row = _log_sigmoid(pre_row)
    pre_col = z_ref[:, if0:if0 + LANES] + bcol_ref[...]
    logf_col = _log_sigmoid(pre_col)

    r_i = lax.broadcasted_iota(jnp.int32, (L, L), 0)
    c_i = lax.broadcasted_iota(jnp.int32, (L, L), 1)
    causal = r_i >= c_i
    tril = causal.astype(F32)
    triu = (r_i <= c_i).astype(F32)

    for c in range(TS // L):
        r0 = c * L
        b_col = _dot_f32(tril, logf_col[r0:r0 + L, :])
        b_row = _dot_f32(logf_row[:, r0:r0 + L], triu)
        for h in range(H):
            q = z_ref[r0:r0 + L, q0 + h * LANES:q0 + (h + 1) * LANES]
            k = z_ref[r0:r0 + L, k0 + h * LANES:k0 + (h + 1) * LANES]
            v = z_ref[r0:r0 + L, v0 + h * LANES:v0 + (h + 1) * LANES]
            o = z_ref[r0:r0 + L, o0 + h * LANES:o0 + (h + 1) * LANES]
            qb, kb, vb = q.astype(BF16), k.astype(BF16), v.astype(BF16)
            bc = b_col[:, H + h:H + h + 1]
            ic = pre_col[r0:r0 + L, h:h + 1]
            br = b_row[H + h:H + h + 1, :]
            ir = pre_row[h:h + 1, r0:r0 + L]
            c_prev = c_ref[h]
            n_prev = n_ref[h]
            m_prev = m_ref[h][:, 0:1]

            dmat = jnp.where(causal, bc - br + ir, -jnp.inf)
            e_inter = bc + m_prev
            m_t = jnp.maximum(e_inter, jnp.max(dmat, axis=-1, keepdims=True))
            w_intra = jnp.exp(dmat - m_t)
            w_inter = jnp.exp(e_inter - m_t)
            s = _dot_nt(qb, kb) * w_intra
            num = w_inter * _dot(qb, c_prev.astype(BF16)) + _dot(s.astype(BF16), vb)
            den = (w_inter * jnp.sum(q * n_prev, axis=-1, keepdims=True)
                   + jnp.sum(s, axis=-1, keepdims=True))
            hh = num / jnp.maximum(jnp.abs(den), jnp.exp(-m_t))

            g = bc[L - 1:L, :]
            a_col = g - bc + ic
            m_new = jnp.maximum(g + m_prev, jnp.max(a_col, axis=0, keepdims=True))
            decay = jnp.exp(g + m_prev - m_new)
            kw = k * jnp.exp(a_col - m_new)
            c_ref[h] = decay * c_prev + _dot_tn(kw.astype(BF16), vb)
            n_ref[h] = decay * n_prev + jnp.sum(kw, axis=0, keepdims=True)
            m_ref[h] = jnp.broadcast_to(m_new, (1, LANES))

            mu = jnp.mean(hh, axis=-1, keepdims=True)
            hc = hh - mu
            var = jnp.mean(hc * hc, axis=-1, keepdims=True)
            hn = hc * lax.rsqrt(var + HEAD_NORM_EPS) * gain_ref[:, h * LANES:(h + 1) * LANES]
            hy_ref[r0:r0 + L, h * LANES:(h + 1) * LANES] = (jax.nn.sigmoid(o) * hn).astype(BF16)

    u = z_ref[:, sc0:sc0 + sconv_w] * z_ref[:, sx0:sx0 + sconv_w]
    ubuf_ref[SUBLANES:SUBLANES + TS, :] = u
    conv = (convw_ref[2:3, :] * u
            + convw_ref[1:2, :] * ubuf_ref[SUBLANES - 1:SUBLANES - 1 + TS, :]
            + convw_ref[0:1, :] * ubuf_ref[SUBLANES - 2:SUBLANES - 2 + TS, :])
    hy_ref[:, HW:HW + sconv_w] = (z_ref[:, sb0:sb0 + sconv_w] * conv).astype(BF16)
    ubuf_ref[0:SUBLANES, :] = ubuf_ref[TS:TS + SUBLANES, :]

    mix = _dot(hy_ref[...], wout_ref[...])
    out_ref[...] = _layer_norm(alpha * x + mix, lng_ref[...], lnb_ref[...])


def _mixer_ab(x, w_in, b_igate, b_fgate, conv_w, head_gain, w_out, ln_g, ln_b, alpha):
    Bn, S, D = x.shape
    H, dk, dv = MLSTM_HEADS, MLSTM_QK_DIM, MLSTM_V_DIM
    assert dv == LANES and dk <= LANES and S % SEQ_TILE == 0 and SEQ_TILE % MLSTM_CHUNK == 0
    hw = H * dv
    sw = D - hw
    cuts = [H * dk, 2 * H * dk, 2 * H * dk + hw, 2 * H * dk + 2 * hw]
    wq, wk, wv, wo = (w_in[:, a:b] for a, b in zip([0] + cuts[:-1], cuts))
    wi = w_in[:, cuts[-1]:cuts[-1] + H]
    wf = w_in[:, cuts[-1] + H:cuts[-1] + 2 * H]
    ws = w_in[:, cuts[-1] + 2 * H:]

    def pad_heads(w, scale):
        w = (w * scale).reshape(D, H, dk)
        return jnp.pad(w, ((0, 0), (0, 0), (0, LANES - dk))).reshape(D, H * LANES)

    w_if = jnp.concatenate([wi, wf], axis=1)
    w_all = jnp.concatenate(
        [pad_heads(wq, 1.0), pad_heads(wk, dk ** -0.5), wv, wo, ws,
         jnp.pad(w_if, ((0, 0), (0, LANES - 2 * H)))], axis=1).astype(BF16)
    nz = w_all.shape[1]
    w_if_t = w_if.T.astype(BF16)
    b_if = jnp.concatenate([b_igate, b_fgate]).astype(F32)
    b_col = jnp.pad(b_if, (0, LANES - 2 * H)).reshape(1, LANES)
    b_row = b_if.reshape(2 * H, 1)

    const = lambda b, j: (0, 0)
    kern = functools.partial(_mixer_ab_kernel, alpha=alpha, sconv_w=sw)
    return pl.pallas_call(
        kern,
        grid=(Bn, S // SEQ_TILE),
        in_specs=[
            pl.BlockSpec((None, SEQ_TILE, D), lambda b, j: (b, j, 0)),
            pl.BlockSpec((D, nz), const),
            pl.BlockSpec((2 * H, D), const),
            pl.BlockSpec((1, LANES), const),
            pl.BlockSpec((2 * H, 1), const),
            pl.BlockSpec((conv_w.shape[0], sw), const),
            pl.BlockSpec((1, hw), const),
            pl.BlockSpec((D, D), const),
            pl.BlockSpec((1, D), const),
            pl.BlockSpec((1, D), const),
        ],
        out_specs=pl.BlockSpec((None, SEQ_TILE, D), lambda b, j: (b, j, 0)),
        out_shape=jax.ShapeDtypeStruct((Bn, S, D), F32),
        scratch_shapes=[
            pltpu.VMEM((SEQ_TILE, nz), F32),
            pltpu.VMEM((SEQ_TILE, D), BF16),
            pltpu.VMEM((SEQ_TILE + 2 * SUBLANES, sw), F32),
            pltpu.VMEM((H, LANES, dv), F32),
            pltpu.VMEM((H, 1, LANES), F32),
            pltpu.VMEM((H, 1, LANES), F32),
        ],
        compiler_params=pltpu.CompilerParams(
            dimension_semantics=("parallel", "arbitrary"), vmem_limit_bytes=VMEM_LIMIT),
        name="mixer_ab",
    )(x, w_all, w_if_t, b_col, b_row, conv_w.astype(F32), head_gain.reshape(1, hw).astype(F32),
      w_out.astype(BF16), ln_g.reshape(1, D), ln_b.reshape(1, D))


def _conformer_kernel(dest_ref, dnext_ref, gate_ref, ys_ref, x_ref, mlng_ref, mlnb_ref,
                      w1_ref, wdw_ref, bdw_ref, cg_ref, cb_ref, w2_ref, lng_ref, lnb_ref,
                      out_ref, ubuf_ref, uc_ref, sh_ref, buf_ref, sems, *, alpha, taps, hist):
    TS = SEQ_TILE
    C = uc_ref.shape[1]

    @pl.when(pl.program_id(1) == 0)
    def _():
        ubuf_ref[0:hist, :] = jnp.zeros((hist, C), F32)

    step = pl.program_id(0) * pl.num_programs(1) + pl.program_id(1)
    n_steps = pl.num_programs(0) * pl.num_programs(1)
    slot = lax.rem(step, 2)

    @pl.when(step == 0)
    def _():
        _fetch_tile_rows(dest_ref, ys_ref, buf_ref, sems, 0, TS)

    _wait_tile_rows(ys_ref, buf_ref, sems, slot, TS)
    f = _gated_sum(gate_ref, buf_ref, slot)
    x = _layer_norm(alpha * x_ref[...] + f, mlng_ref[...], mlnb_ref[...])
    z = _dot(x.astype(BF16), w1_ref[...])
    ubuf_ref[hist:hist + TS, :] = z[:, :C] * jax.nn.sigmoid(z[:, C:])

    base = hist - (taps - 1)
    sh_rows = sh_ref.shape[1]
    n_conv_blocks = (C // LANES) * (TS // CONV_ROWS)
    fetch_per_block = TS // n_conv_blocks
    assert fetch_per_block * n_conv_blocks == TS
    for cb in range(C // LANES):
        cs = slice(cb * LANES, (cb + 1) * LANES)
        for r in range(1, SUBLANES):
            sh_ref[r - 1] = ubuf_ref[r:r + sh_rows, cs]
        for rb in range(TS // CONV_ROWS):
            t0 = (cb * (TS // CONV_ROWS) + rb) * fetch_per_block
            for t in range(t0, t0 + fetch_per_block):
                _fetch_token_rows(dnext_ref, ys_ref, buf_ref, sems, 1 - slot, t, TS)
            acc = jnp.zeros((CONV_ROWS, LANES), F32) + bdw_ref[:, cs]
            for j in range(taps):
                r = (base + j) % SUBLANES
                start = rb * CONV_ROWS + base + j - r
                if r == 0:
                    win = ubuf_ref[start:start + CONV_ROWS, cs]
                else:
                    win = sh_ref[r - 1, start:start + CONV_ROWS, :]
                acc = acc + wdw_ref[j:j + 1, cs] * win
            uc_ref[rb * CONV_ROWS:(rb + 1) * CONV_ROWS, cs] = acc
    ubuf_ref[0:hist, :] = ubuf_ref[TS:TS + hist, :]

    un = _layer_norm(uc_ref[...], cg_ref[...], cb_ref[...])
    act = un * jax.nn.sigmoid(un)
    mix = _dot(act.astype(BF16), w2_ref[...])
    out_ref[...] = _layer_norm(alpha * x + mix, lng_ref[...], lnb_ref[...])

    @pl.when(step == n_steps - 1)
    def _():
        _wait_tile_rows(ys_ref, buf_ref, sems, 1 - slot, TS)


def _conformer(x, moe_out, moe_ln_g, moe_ln_b, w_pw1, w_dw, b_dw, cg, cb, w_pw2, ln_g, ln_b, alpha):
    ys, dest, gates_col = moe_out
    Bn, S, D = x.shape
    taps, C = w_dw.shape
    hist = -(-(taps - 1) // SUBLANES) * SUBLANES
    assert S % SEQ_TILE == 0 and C % LANES == 0 and hist <= SEQ_TILE
    nj = S // SEQ_TILE
    const = lambda b, j: (0, 0)
    dest_tiles = _tile_slots(dest, SEQ_TILE)
    gather_specs, gather_scratch = _moe_gather_specs(SEQ_TILE, Bn * nj, lambda b, j: b * nj + j)
    kern = functools.partial(_conformer_kernel, alpha=alpha, taps=taps, hist=hist)
    return pl.pallas_call(
        kern,
        grid=(Bn, nj),
        in_specs=gather_specs + [
            pl.BlockSpec((None, SEQ_TILE, D), lambda b, j: (b, j, 0)),
            pl.BlockSpec((1, D), const),
            pl.BlockSpec((1, D), const),
            pl.BlockSpec((D, 2 * C), const),
            pl.BlockSpec((taps, C), const),
            pl.BlockSpec((1, C), const),
            pl.BlockSpec((1, C), const),
            pl.BlockSpec((1, C), const),
            pl.BlockSpec((C, D), const),
            pl.BlockSpec((1, D), const),
            pl.BlockSpec((1, D), const),
        ],
        out_specs=pl.BlockSpec((None, SEQ_TILE, D), lambda b, j: (b, j, 0)),
        out_shape=jax.ShapeDtypeStruct((Bn, S, D), F32),
        scratch_shapes=[
            pltpu.VMEM((SEQ_TILE + hist, C), F32),
            pltpu.VMEM((SEQ_TILE, C), F32),
            pltpu.VMEM((SUBLANES - 1, SEQ_TILE + hist - SUBLANES, LANES), F32),
        ] + gather_scratch,
        compiler_params=pltpu.CompilerParams(
            dimension_semantics=("arbitrary", "arbitrary"), vmem_limit_bytes=VMEM_LIMIT),
        name="conformer",
    )(dest_tiles, dest_tiles, gates_col, ys, x, moe_ln_g.reshape(1, D), moe_ln_b.reshape(1, D),
      w_pw1.astype(BF16), w_dw.astype(F32), b_dw.reshape(1, C), cg.reshape(1, C),
      cb.reshape(1, C), w_pw2.astype(BF16), ln_g.reshape(1, D), ln_b.reshape(1, D))


def _router_kernel(x_ref, rwt_ref, rb_ref, idx_ref, gate_ref, rank_ref, cnt_ref, base_ref):
    E = rwt_ref.shape[0]
    TT = x_ref.shape[0]

    @pl.when(pl.program_id(0) == 0)
    def _():
        base_ref[...] = jnp.zeros_like(base_ref)

    logits = lax.dot_general(rwt_ref[...], x_ref[...], (((1,), (1,)), ((), ())),
                             preferred_element_type=F32,
                             precision=lax.Precision.HIGHEST) + rb_ref[...]
    e_iota = lax.broadcasted_iota(jnp.int32, (E, TT), 0)
    work = logits
    vals, idxs, hots = [], [], []
    for _ in range(TOP_K):
        mx = jnp.max(work, axis=0, keepdims=True)
        ix = jnp.min(jnp.where(work == mx, e_iota, E), axis=0, keepdims=True)
        hot = e_iota == ix
        work = jnp.where(hot, -jnp.inf, work)
        vals.append(mx)
        idxs.append(ix)
        hots.append(hot)
    ex = [jnp.exp(v - vals[0]) for v in vals]
    tot = functools.reduce(lambda a, b: a + b, ex)

    member = functools.reduce(lambda a, b: a | b, hots).astype(F32)
    s_i = lax.broadcasted_iota(jnp.int32, (TT, TT), 0)
    t_i = lax.broadcasted_iota(jnp.int32, (TT, TT), 1)
    before = (s_i < t_i).astype(BF16)
    pos = _dot(member.astype(BF16), before) + base_ref[:, 0:1]
    for kk in range(TOP_K):
        idx_ref[kk:kk + 1, :] = idxs[kk]
        gate_ref[kk:kk + 1, :] = ex[kk] / tot
        rank_ref[kk:kk + 1, :] = jnp.sum(jnp.where(hots[kk], pos, 0.0), axis=0,
                                         keepdims=True).astype(jnp.int32)
    base_ref[...] = base_ref[...] + jnp.sum(member, axis=1, keepdims=True)
    cnt_ref[...] = base_ref[...].astype(jnp.int32)


def _router(xt, router_w, router_b):
    T, D = xt.shape
    E = router_w.shape[1]
    assert T % ROUTER_TILE == 0
    tok = lambda i: (0, i)
    return pl.pallas_call(
        _router_kernel,
        grid=(T // ROUTER_TILE,),
        in_specs=[
            pl.BlockSpec((ROUTER_TILE, D), lambda i: (i, 0)),
            pl.BlockSpec((E, D), lambda i: (0, 0)),
            pl.BlockSpec((E, 1), lambda i: (0, 0)),
        ],
        out_specs=[
            pl.BlockSpec((TOP_K, ROUTER_TILE), tok),
            pl.BlockSpec((TOP_K, ROUTER_TILE), tok),
            pl.BlockSpec((TOP_K, ROUTER_TILE), tok),
            pl.BlockSpec((E, LANES), lambda i: (0, 0)),
        ],
        out_shape=[
            jax.ShapeDtypeStruct((TOP_K, T), jnp.int32),
            jax.ShapeDtypeStruct((TOP_K, T), F32),
            jax.ShapeDtypeStruct((TOP_K, T), jnp.int32),
            jax.ShapeDtypeStruct((E, LANES), jnp.int32),
        ],
        scratch_shapes=[pltpu.VMEM((E, LANES), F32)],
        compiler_params=pltpu.CompilerParams(
            dimension_semantics=("arbitrary",), vmem_limit_bytes=VMEM_LIMIT),
        name="router",
    )(xt, router_w.T.astype(F32), router_b.reshape(E, 1).astype(F32))


def _row_tile(i):
    if isinstance(i, int):
        return pl.ds(i * SUBLANES, SUBLANES)
    return pl.ds(pl.multiple_of(i * SUBLANES, SUBLANES), SUBLANES)


def _to_row_tiles(dst_ref, val):
    n = val.shape[0]
    for s in range(SUBLANES):
        dst_ref[pl.ds(s, n, stride=SUBLANES), :] = val[:, s * LANES:(s + 1) * LANES]


def _from_row_tiles(src_ref, n):
    return jnp.concatenate(
        [src_ref[pl.ds(s, n, stride=SUBLANES), :] for s in range(SUBLANES)], axis=-1)


def _dispatch_kernel(pad_start_ref, pad_cnt_ref, nbu_ref, dest_ref, x_ref, xs_ref, src_ref, zblk_ref,
                     sems):
    TT = x_ref.shape[0]
    E = pad_start_ref.shape[0]
    RT = zblk_ref.shape[0]
    nb = xs_ref.shape[0] // RT

    _to_row_tiles(src_ref, x_ref[...])

    def issue(g, carry):
        for u in range(ISSUE_UNROLL):
            t = g * ISSUE_UNROLL + u
            for kk in range(TOP_K):
                d = dest_ref[0, 0, kk * TT + t]
                q = kk % DMA_QUEUES
                pltpu.make_async_copy(src_ref.at[_row_tile(t), :], xs_ref.at[_row_tile(d), :],
                                      sems.at[q]).start(priority=q)
        return carry

    lax.fori_loop(0, TT // ISSUE_UNROLL, issue, 0)
    for kk in range(TOP_K):
        pltpu.make_async_copy(src_ref, xs_ref.at[pl.ds(0, TT * SUBLANES), :],
                              sems.at[kk % DMA_QUEUES]).wait()

    @pl.when(pl.program_id(0) == pl.num_programs(0) - 1)
    def _():
        zblk_ref[...] = jnp.zeros_like(zblk_ref)
        sem = sems.at[0]

        def for_each_pad(fn):
            def per_expert(e, carry):
                n = pad_cnt_ref[e]
                row = pad_start_ref[e]
                p = RT // SUBLANES // 2
                while p >= 1:
                    take = (n & p) != 0
                    at = row

                    @pl.when(take)
                    def _():
                        fn(pltpu.make_async_copy(
                            zblk_ref.at[pl.ds(0, p * SUBLANES), :],
                            xs_ref.at[pl.ds(pl.multiple_of(at * SUBLANES, SUBLANES), p * SUBLANES), :],
                            sem))

                    row = row + jnp.where(take, p, 0)
                    p //= 2
                return carry
            lax.fori_loop(0, E, per_expert, 0)

            def per_block(blk, carry):
                fn(pltpu.make_async_copy(
                    zblk_ref, xs_ref.at[pl.ds(pl.multiple_of(blk * RT, RT), RT), :], sem))
                return carry
            lax.fori_loop(nbu_ref[0], nb, per_block, 0)

        for_each_pad(lambda cp: cp.start())
        for_each_pad(lambda cp: cp.wait())


def _dispatch(xt, dest_tiles, pad_start, pad_cnt, nb_used, n_rows):
    T, D = xt.shape
    nt = T // DISPATCH_TILE
    grid_spec = pltpu.PrefetchScalarGridSpec(
        num_scalar_prefetch=3,
        grid=(nt,),
        in_specs=[
            pl.BlockSpec((1, 1, TOP_K * DISPATCH_TILE), lambda i, ps, pc, nbu: (i, 0, 0),
                         memory_space=pltpu.SMEM),
            pl.BlockSpec((DISPATCH_TILE, D), lambda i, ps, pc, nbu: (i, 0)),
        ],
        out_specs=pl.BlockSpec(memory_space=pl.ANY),
        scratch_shapes=[pltpu.VMEM((DISPATCH_TILE * SUBLANES, LANES), F32),
                        pltpu.VMEM((EXPERT_ROWS * SUBLANES, LANES), F32),
                        pltpu.SemaphoreType.DMA((DMA_QUEUES,))],
    )
    assert D == SUBLANES * LANES
    return pl.pallas_call(
        _dispatch_kernel,
        grid_spec=grid_spec,
        out_shape=jax.ShapeDtypeStruct((n_rows * SUBLANES, LANES), F32),
        compiler_params=pltpu.CompilerParams(
            dimension_semantics=("arbitrary",), vmem_limit_bytes=VMEM_LIMIT),
        name="dispatch",
    )(pad_start, pad_cnt, nb_used, dest_tiles, xt)


def _cast_rows(src_ref, dst_ref):
    rows = src_ref.shape[0]

    def body(i, carry):
        r = pl.multiple_of(i * CAST_ROWS, CAST_ROWS)
        dst_ref[pl.ds(r, CAST_ROWS), :] = src_ref[pl.ds(r, CAST_ROWS), :].astype(dst_ref.dtype)
        return carry

    lax.fori_loop(0, rows // CAST_ROWS, body, 0)


def _expert_kernel(ge_ref, nbu_ref, xs_ref, wu_ref, bu_ref, wd_ref, bd_ref, ys_ref, wub_ref, wdb_ref):
    F = wd_ref.shape[0]
    b = pl.program_id(0)

    @pl.when(b >= nbu_ref[0])
    def _():
        ys_ref[...] = jnp.zeros_like(ys_ref)

    @pl.when(b < nbu_ref[0])
    def _():
        @pl.when((b == 0) | (ge_ref[b] != ge_ref[jnp.maximum(b - 1, 0)]))
        def _():
            _cast_rows(wu_ref, wub_ref)
            _cast_rows(wd_ref, wdb_ref)

        xs = _from_row_tiles(xs_ref, EXPERT_ROWS)
        hb = _dot(xs.astype(BF16), wub_ref[...]) + bu_ref[...]
        glu = jnp.minimum(hb[:, :F], SWIGLU_LIMIT)
        lin = jnp.clip(hb[:, F:], -SWIGLU_LIMIT, SWIGLU_LIMIT)
        act = glu * jax.nn.sigmoid(SWIGLU_ALPHA * glu) * (lin + 1.0)
        _to_row_tiles(ys_ref, _dot(act.astype(BF16), wdb_ref[...]) + bd_ref[...])


def _experts(xs, group_expert, nb_used, layer, w_up, b_up, w_down, b_down):
    depth, E, D, F2 = w_up.shape
    F = w_down.shape[2]
    RT = EXPERT_ROWS * SUBLANES
    nb = xs.shape[0] // RT
    assert D % CAST_ROWS == 0 and F % CAST_ROWS == 0
    row_block = lambda b, ge, nbu: (jnp.minimum(b, nbu[0] - 1), 0)
    expert_block = lambda b, ge, nbu: (layer, ge[b], 0, 0)
    grid_spec = pltpu.PrefetchScalarGridSpec(
        num_scalar_prefetch=2,
        grid=(nb,),
        in_specs=[
            pl.BlockSpec((RT, LANES), row_block),
            pl.BlockSpec((None, None, D, F2), expert_block),
            pl.BlockSpec((None, None, 1, F2), expert_block),
            pl.BlockSpec((None, None, F, D), expert_block),
            pl.BlockSpec((None, None, 1, D), expert_block),
        ],
        out_specs=pl.BlockSpec((RT, LANES), lambda b, ge, nbu: (b, 0)),
        scratch_shapes=[pltpu.VMEM((D, F2), BF16), pltpu.VMEM((F, D), BF16)],
    )
    return pl.pallas_call(
        _expert_kernel,
        grid_spec=grid_spec,
        out_shape=jax.ShapeDtypeStruct(xs.shape, F32),
        compiler_params=pltpu.CompilerParams(
            dimension_semantics=("arbitrary",), vmem_limit_bytes=VMEM_LIMIT),
        name="experts",
    )(group_expert, nb_used, xs, w_up, b_up.reshape(depth, E, 1, F2), w_down,
      b_down.reshape(depth, E, 1, D))


def _fetch_token_rows(idx_ref, ys_ref, buf_ref, sems, s, t, TT):
    for kk in range(TOP_K):
        d = idx_ref[0, 0, kk * TT + t]
        q = kk % DMA_QUEUES
        pltpu.make_async_copy(ys_ref.at[_row_tile(d), :], buf_ref.at[s, kk, _row_tile(t), :],
                              sems.at[s, q]).start(priority=q)


def _fetch_tile_rows(idx_ref, ys_ref, buf_ref, sems, s, TT):
    def body(g, carry):
        for u in range(ISSUE_UNROLL):
            _fetch_token_rows(idx_ref, ys_ref, buf_ref, sems, s, g * ISSUE_UNROLL + u, TT)
        return carry
    lax.fori_loop(0, TT // ISSUE_UNROLL, body, 0)


def _wait_tile_rows(ys_ref, buf_ref, sems, s, TT):
    for kk in range(TOP_K):
        pltpu.make_async_copy(ys_ref.at[pl.ds(0, TT * SUBLANES), :], buf_ref.at[s, kk],
                              sems.at[s, kk % DMA_QUEUES]).wait()


def _gated_sum(gate_ref, buf_ref, s):
    TT = gate_ref.shape[0]
    f = gate_ref[:, 0:1] * _from_row_tiles(buf_ref.at[s, 0], TT)
    for kk in range(1, TOP_K):
        f = f + gate_ref[:, kk:kk + 1] * _from_row_tiles(buf_ref.at[s, kk], TT)
    return f


def _moe_output(step, n_steps, dest_ref, dnext_ref, gate_ref, ys_ref, buf_ref, sems):
    TT = gate_ref.shape[0]
    slot = lax.rem(step, 2)

    @pl.when(step == 0)
    def _():
        _fetch_tile_rows(dest_ref, ys_ref, buf_ref, sems, 0, TT)

    @pl.when(step + 1 < n_steps)
    def _():
        _fetch_tile_rows(dnext_ref, ys_ref, buf_ref, sems, 1 - slot, TT)

    _wait_tile_rows(ys_ref, buf_ref, sems, slot, TT)
    return _gated_sum(gate_ref, buf_ref, slot)


def _moe_gather_specs(tile, n_tiles, step_of):
    slots = (1, 1, TOP_K * tile)
    in_specs = [
        pl.BlockSpec(slots, lambda *g: (step_of(*g), 0, 0), memory_space=pltpu.SMEM),
        pl.BlockSpec(slots, lambda *g: (jnp.minimum(step_of(*g) + 1, n_tiles - 1), 0, 0),
                     memory_space=pltpu.SMEM),
        pl.BlockSpec((tile, TOP_K), lambda *g: (step_of(*g), 0)),
        pl.BlockSpec(memory_space=pl.ANY),
    ]
    scratch = [pltpu.VMEM((2, TOP_K, tile * SUBLANES, LANES), F32),
               pltpu.SemaphoreType.DMA((2, DMA_QUEUES))]
    return in_specs, scratch


def _combine_kernel(dest_ref, dnext_ref, gate_ref, ys_ref, x_ref, lng_ref, lnb_ref, out_ref,
                    buf_ref, sems, *, alpha):
    f = _moe_output(pl.program_id(0), pl.num_programs(0), dest_ref, dnext_ref, gate_ref, ys_ref,
                    buf_ref, sems)
    out_ref[...] = _layer_norm(alpha * x_ref[...] + f, lng_ref[...], lnb_ref[...])


def _combine(xt, moe_out, ln_g, ln_b, alpha):
    ys, dest, gates_col = moe_out
    T, D = xt.shape
    nt = T // COMBINE_TILE
    dest_tiles = _tile_slots(dest, COMBINE_TILE)
    gather_specs, gather_scratch = _moe_gather_specs(COMBINE_TILE, nt, lambda i: i)
    kern = functools.partial(_combine_kernel, alpha=alpha)
    return pl.pallas_call(
        kern,
        grid=(nt,),
        in_specs=gather_specs + [
            pl.BlockSpec((COMBINE_TILE, D), lambda i: (i, 0)),
            pl.BlockSpec((1, D), lambda i: (0, 0)),
            pl.BlockSpec((1, D), lambda i: (0, 0)),
        ],
        out_specs=pl.BlockSpec((COMBINE_TILE, D), lambda i: (i, 0)),
        out_shape=jax.ShapeDtypeStruct((T, D), F32),
        scratch_shapes=gather_scratch,
        compiler_params=pltpu.CompilerParams(
            dimension_semantics=("arbitrary",), vmem_limit_bytes=VMEM_LIMIT),
        name="combine",
    )(dest_tiles, dest_tiles, gates_col, ys, xt, ln_g.reshape(1, D), ln_b.reshape(1, D))


def _tile_slots(dest, tile):
    K, T = dest.shape
    return dest.reshape(K, T // tile, tile).transpose(1, 0, 2).reshape(T // tile, 1, K * tile)


def _moe_experts(xt, layer, router_w, router_b, w_up, b_up, w_down, b_down):
    T, D = xt.shape
    E = router_w.shape[1]
    R = EXPERT_ROWS
    idx, gates, rank, cnt = _router(xt, router_w, router_b)
    counts = cnt[:, 0]
    padded = ((counts + R - 1) // R) * R
    pends = jnp.cumsum(padded)
    pstarts = pends - padded
    e_ids = jnp.arange(E, dtype=jnp.int32)[:, None, None]
    dest = rank + jnp.sum(jnp.where(idx[None] == e_ids, pstarts[:, None, None], 0), axis=0)
    n_rows = -(-(T * TOP_K + E * (R - 1)) // R) * R
    nb = n_rows // R
    nb_used = (pends[E - 1:] // R).astype(jnp.int32)
    blk = jnp.minimum(jnp.arange(nb, dtype=jnp.int32), nb_used - 1) * R
    group_expert = jnp.sum((pends[None, :] <= blk[:, None]).astype(jnp.int32), axis=1)
    xs = _dispatch(xt, _tile_slots(dest, DISPATCH_TILE), (pstarts + counts).astype(jnp.int32),
                   (padded - counts).astype(jnp.int32), nb_used, n_rows)
    ys = _experts(xs, group_expert, nb_used, layer, w_up, b_up, w_down, b_down)
    return ys, dest, gates.T


def kernel(x, ab_w_in, ab_b_igate, ab_b_fgate, ab_conv_w, ab_head_gain, ab_w_out, cf_w_pw1, cf_w_dw,
           cf_b_dw, cf_ln_g, cf_ln_b, cf_w_pw2, router_w, router_b, exp_w_up, exp_b_up, exp_w_down,
           exp_b_down, post_ln_g, post_ln_b):
    Bn, S, D = x.shape
    depth = post_ln_g.shape[0]
    alpha = float((2 * depth) ** 0.25)
    moe_out = None
    for layer in range(depth):
        j = layer // 2
        if layer % 2 == 0:
            if moe_out is not None:
                x = _combine(x.reshape(Bn * S, D), moe_out, post_ln_g[layer - 1, 1],
                             post_ln_b[layer - 1, 1], alpha).reshape(Bn, S, D)
            x = _mixer_ab(x, ab_w_in[j], ab_b_igate[j], ab_b_fgate[j], ab_conv_w[j], ab_head_gain[j],
                          ab_w_out[j], post_ln_g[layer, 0], post_ln_b[layer, 0], alpha)
        else:
            x = _conformer(x, moe_out, post_ln_g[layer - 1, 1], post_ln_b[layer - 1, 1], cf_w_pw1[j],
                           cf_w_dw[j], cf_b_dw[j], cf_ln_g[j], cf_ln_b[j], cf_w_pw2[j],
                           post_ln_g[layer, 0], post_ln_b[layer, 0], alpha)
        moe_out = _moe_experts(x.reshape(Bn * S, D), layer, router_w[layer], router_b[layer],
                               exp_w_up, exp_b_up, exp_w_down, exp_b_down)
    x = _combine(x.reshape(Bn * S, D), moe_out, post_ln_g[depth - 1, 1], post_ln_b[depth - 1, 1],
                 alpha)
    return x.reshape(Bn, S, D)
```
